```python
import jax, jax.numpy as jnp
from jax import lax
import numpy as np

D_MODEL = 1024
BATCH = 8
SEQ = 8192
DEPTH = 2

PLE_DIM = 256
A_HEADS = 8
A_HEAD_DIM = 64
A_WIDTH = A_HEADS * A_HEAD_DIM
A_KV_DIM = 64
IDX_HEADS = 8
IDX_DIM = 64
TOPK_MAX = 256
Q_BLOCK = 128
B_HEADS = 4
B_KEY_DIM = 128
B_VAL_DIM = 256
B_KEY_WIDTH = B_HEADS * B_KEY_DIM
B_VAL_WIDTH = B_HEADS * B_VAL_DIM
GATE_RANK = 16
GATE_TAU = 16.0
CHUNK = 64
EPS = 1e-6

IN_SPLITS = (A_WIDTH, A_KV_DIM, A_KV_DIM, IDX_HEADS * IDX_DIM, IDX_DIM, IDX_HEADS, A_WIDTH,
             B_KEY_WIDTH, B_KEY_WIDTH, B_VAL_WIDTH, GATE_RANK, B_VAL_WIDTH,
             D_MODEL, D_MODEL)
IN_WIDTH = sum(IN_SPLITS)

kernel_name = "hybrid_dsa_gla_gated_merge"

F32 = jnp.float32


def rms_norm(x, g):
    x32 = x.astype(F32)
    y = x32 * lax.rsqrt(jnp.mean(x32 * x32, axis=-1, keepdims=True) + EPS)
    return (y * g.astype(F32)).astype(x.dtype)


def dsa_attention(q, k, v, q_idx, k_idx, w_idx):
    B, L = q.shape[0], q.shape[1]
    top_k = min(TOPK_MAX, L // 4)
    nb = L // Q_BLOCK
    kv = jnp.concatenate([k, v], axis=-1)
    k_idx32 = k_idx.astype(F32)
    key_pos = jnp.arange(L)
    idx_scale = (IDX_HEADS ** -0.5) * (IDX_DIM ** -0.5)
    attn_scale = A_HEAD_DIM ** -0.5

    def to_blocks(t):
        return jnp.moveaxis(t.reshape((B, nb, Q_BLOCK) + t.shape[2:]), 1, 0)

    def block(args):
        qb, qib, wb, start = args
        q_pos = start + jnp.arange(Q_BLOCK)
        causal = key_pos[None, :] <= q_pos[:, None]
        logits = jnp.einsum('bqhd,bsd->bqhs', qib.astype(F32), k_idx32)
        score = jnp.einsum('bqh,bqhs->bqs', wb.astype(F32) * idx_scale, jax.nn.relu(logits))
        score = jnp.where(causal[None], score, -jnp.inf)
        _, idx = lax.top_k(score, top_k)
        sel = jax.vmap(lambda t, i: t[i])(kv, idx)
        k_sel, v_sel = jnp.split(sel, 2, axis=-1)
        valid = idx <= q_pos[None, :, None]
        s = jnp.einsum('bqhd,bqkd->bqhk', qb, k_sel).astype(F32) * attn_scale
        s = jnp.where(valid[:, :, None, :], s, -jnp.inf)
        prob = jax.nn.softmax(s, axis=-1).astype(v_sel.dtype)
        return jnp.einsum('bqhk,bqkd->bqhd', prob, v_sel)

    starts = jnp.arange(nb) * Q_BLOCK
    out = lax.map(block, (to_blocks(q), to_blocks(q_idx), to_blocks(w_idx), starts))
    return jnp.moveaxis(out, 0, 1).reshape(B, L, A_HEADS * A_KV_DIM)


def gla_chunked(q, k, v, log_a):
    B, L, H, dk = q.shape
    dv = v.shape[-1]
    n = L // CHUNK

    def chunks(t):
        return t.astype(F32).reshape(B, n, CHUNK, H, t.shape[-1]).transpose(1, 0, 3, 2, 4)

    causal = jnp.tril(jnp.ones((CHUNK, CHUNK), dtype=bool))

    def step(S, inp):
        qc, kc, vc, ac = inp
        G = jnp.cumsum(ac, axis=2)
        diff = G[:, :, :, None, :] - G[:, :, None, :, :]
        decay = jnp.exp(jnp.where(causal[None, None, :, :, None], diff, -jnp.inf))
        A = jnp.sum(qc[:, :, :, None, :] * kc[:, :, None, :, :] * decay, axis=-1)
        o = (jnp.einsum('bhij,bhje->bhie', A, vc)
             + jnp.einsum('bhid,bhde->bhie', qc * jnp.exp(G), S))
        G_last = G[:, :, -1:, :]
        S = (jnp.exp(G_last[:, :, 0, :])[..., None] * S
             + jnp.einsum('bhcd,bhce->bhde', kc * jnp.exp(G_last - G), vc))
        return S, o

    S0 = jnp.zeros((B, H, dk, dv), F32)
    _, o = lax.scan(step, S0, (chunks(q), chunks(k), chunks(v), chunks(log_a)))
    return o.transpose(1, 0, 3, 2, 4).reshape(B, L, H, dv)


def hybrid_layer(x, p_i, g_pre, w_in, w_gate_up, b_gate, g_gla_head, w_proj_a, w_proj_b,
                 w_out, g_post, w_ple, w_ple_gate, g_ple_pre, g_ple_post):
    B, L, _ = x.shape
    h = rms_norm(x, g_pre)
    z = h @ w_in
    (qa, ka, va, qi, ki, wi, ga, qb, kb, vb, gdown, gb, ma, mb) = jnp.split(
        z, np.cumsum(IN_SPLITS)[:-1].tolist(), axis=-1)

    oa = dsa_attention(qa.reshape(B, L, A_HEADS, A_HEAD_DIM), ka, va,
                       qi.reshape(B, L, IDX_HEADS, IDX_DIM), ki, wi)
    oa = oa.astype(x.dtype) * jax.nn.silu(ga)

    log_a = jax.nn.log_sigmoid((gdown @ w_gate_up + b_gate).astype(F32)) / GATE_TAU
    ob = gla_chunked(qb.reshape(B, L, B_HEADS, B_KEY_DIM) * (B_KEY_DIM ** -0.5),
                     kb.reshape(B, L, B_HEADS, B_KEY_DIM),
                     vb.reshape(B, L, B_HEADS, B_VAL_DIM),
                     log_a.reshape(B, L, B_HEADS, B_KEY_DIM))
    ob = rms_norm(ob.astype(x.dtype), g_gla_head).reshape(B, L, B_VAL_WIDTH) * jax.nn.silu(gb)

    y = jax.nn.sigmoid(ma) * (oa @ w_proj_a) + jax.nn.sigmoid(mb) * (ob @ w_proj_b)
    x = x + rms_norm(y @ w_out, g_post)

    e = (p_i @ w_ple) * jax.nn.sigmoid(rms_norm(x, g_ple_pre) @ w_ple_gate)
    return x + rms_norm(e, g_ple_post)


def setup_inputs(seed: int = 0) -> dict:
    key = jax.random.key(seed)
    ks = jax.random.split(key, 16)

    def w(k, shape, fan_in):
        return jax.random.normal(k, shape, F32) * (fan_in ** -0.5)

    def gain(k, shape):
        return 1.0 + 0.05 * jax.random.normal(k, shape, F32)

    return {
        "x": jax.random.normal(ks[0], (BATCH, SEQ, D_MODEL), F32),
        "p": jax.random.normal(ks[1], (DEPTH, BATCH, SEQ, PLE_DIM), F32),
        "g_pre": gain(ks[2], (DEPTH, D_MODEL)),
        "w_in": w(ks[3], (DEPTH, D_MODEL, IN_WIDTH), D_MODEL),
        "w_gate_up": w(ks[4], (DEPTH, GATE_RANK, B_KEY_WIDTH), GATE_RANK),
        "b_gate": 0.1 * jax.random.normal(ks[5], (DEPTH, B_KEY_WIDTH), F32),
        "g_gla_head": gain(ks[6], (DEPTH, B_VAL_DIM)),
        "w_proj_a": w(ks[7], (DEPTH, A_WIDTH, D_MODEL), A_WIDTH),
        "w_proj_b": w(ks[8], (DEPTH, B_VAL_WIDTH, D_MODEL), B_VAL_WIDTH),
        "w_out": w(ks[9], (DEPTH, D_MODEL, D_MODEL), D_MODEL),
        "g_post": gain(ks[10], (DEPTH, D_MODEL)),
        "w_ple": w(ks[11], (DEPTH, PLE_DIM, D_MODEL), PLE_DIM),
        "w_ple_gate": w(ks[12], (DEPTH, D_MODEL, D_MODEL), D_MODEL),
        "g_ple_pre": gain(ks[13], (DEPTH, D_MODEL)),
        "g_ple_post": gain(ks[14], (DEPTH, D_MODEL)),
    }


def reference(x, p, g_pre, w_in, w_gate_up, b_gate, g_gla_head, w_proj_a, w_proj_b,
              w_out, g_post, w_ple, w_ple_gate, g_ple_pre, g_ple_post):
    for i in range(DEPTH):
        x = hybrid_layer(x, p[i], g_pre[i], w_in[i], w_gate_up[i], b_gate[i], g_gla_head[i],
                         w_proj_a[i], w_proj_b[i], w_out[i], g_post[i], w_ple[i],
                         w_ple_gate[i], g_ple_pre[i], g_ple_post[i])
    return x
```

```python
import functools

import jax
import jax.numpy as jnp
from jax import lax
from jax.experimental import pallas as pl
from jax.experimental.pallas import tpu as pltpu

F32 = jnp.float32
BF16 = jnp.bfloat16

D_MODEL = 1024
PLE_DIM = 256
A_HEADS = 8
A_HEAD_DIM = 64
A_WIDTH = A_HEADS * A_HEAD_DIM
A_KV_DIM = 64
IDX_HEADS = 8
IDX_DIM = 64
TOPK_MAX = 256
B_HEADS = 4
B_KEY_DIM = 128
B_VAL_DIM = 256
B_KEY_WIDTH = B_HEADS * B_KEY_DIM
B_VAL_WIDTH = B_HEADS * B_VAL_DIM
GATE_RANK = 16
GATE_TAU = 16.0
EPS = 1e-6

IN_SPLITS = (A_WIDTH, A_KV_DIM, A_KV_DIM, IDX_HEADS * IDX_DIM, IDX_DIM, IDX_HEADS, A_WIDTH,
             B_KEY_WIDTH, B_KEY_WIDTH, B_VAL_WIDTH, GATE_RANK, B_VAL_WIDTH,
             D_MODEL, D_MODEL)

VMEM_LIMIT_BYTES = 56 * 1024 * 1024
LANES = 128

SMALL_WI_OFF = 0
SMALL_GD_OFF = IDX_HEADS

IN_PROJ_ROWS = 256
MERGE_ROWS = 256
DSA_Q_ROWS = 128
DSA_KEY_CHUNK = 512
GLA_CHUNK = 128
GLA_SUB = 16

NT_DIMS = (((1,), (1,)), ((), ()))


def _sigmoid(x):
    return 1.0 / (1.0 + jnp.exp(-x))


def _rms(x, g):
    ms = jnp.mean(x * x, axis=-1, keepdims=True)
    return x * lax.rsqrt(ms + EPS) * g


def _params(*sem):
    return pltpu.CompilerParams(dimension_semantics=sem, vmem_limit_bytes=VMEM_LIMIT_BYTES)


def _in_proj_kernel(x_ref, g_ref, *refs, n_out):
    w_refs, o_refs = refs[:n_out], refs[n_out:]
    h = _rms(x_ref[...], g_ref[...]).astype(BF16)
    for w_ref, o_ref in zip(w_refs, o_refs):
        o_ref[...] = jnp.dot(h, w_ref[...], preferred_element_type=F32).astype(o_ref.dtype)


def _in_proj(x2d, g_pre, w_groups, out_dtypes):
    T = x2d.shape[0]
    tm = IN_PROJ_ROWS
    n_out = len(w_groups)
    in_specs = [pl.BlockSpec((tm, D_MODEL), lambda i: (i, 0)),
                pl.BlockSpec((1, D_MODEL), lambda i: (0, 0))]
    in_specs += [pl.BlockSpec(w.shape, lambda i: (0, 0)) for w in w_groups]
    out_specs = [pl.BlockSpec((tm, w.shape[1]), lambda i: (i, 0)) for w in w_groups]
    out_shape = [jax.ShapeDtypeStruct((T, w.shape[1]), dt) for w, dt in zip(w_groups, out_dtypes)]
    return pl.pallas_call(
        functools.partial(_in_proj_kernel, n_out=n_out),
        grid=(T // tm,),
        in_specs=in_specs,
        out_specs=out_specs,
        out_shape=out_shape,
        compiler_params=_params("arbitrary"),
        name="in_proj",
    )(x2d, g_pre.reshape(1, D_MODEL), *w_groups)


def _key_to_f32(u):
    key = u ^ jnp.int32(-2 ** 31)
    bits = jnp.where(key >= 0, key, key ^ jnp.int32(0x7FFFFFFF))
    return lax.bitcast_convert_type(bits, F32)


def _dsa_kernel(qi_ref, small_ref, qa_ref, ga_ref, ki_ref, ka_ref, va_ref, o_ref,
                qis_ref, qas_ref, vext_ref, sc_ref, m_ref, alpha_ref, acc_ref, p_ref,
                *, tq, ck, top_k):
    qb = pl.program_id(1)
    nh = A_HEADS
    idx_scale = (IDX_HEADS ** -0.5) * (IDX_DIM ** -0.5)
    attn_scale = A_HEAD_DIM ** -0.5

    @pl.when(qb == 0)
    def _():
        vext_ref[:, 0:A_KV_DIM] = va_ref[...]
        vext_ref[:, A_KV_DIM:LANES] = jnp.ones((vext_ref.shape[0], LANES - A_KV_DIM), BF16)

    for h in range(nh):
        qis_ref[h * tq:(h + 1) * tq, :] = qi_ref[:, h * IDX_DIM:(h + 1) * IDX_DIM]
        qa_h = qa_ref[:, h * A_HEAD_DIM:(h + 1) * A_HEAD_DIM].astype(F32) * attn_scale
        qas_ref[h * tq:(h + 1) * tq, :] = qa_h.astype(BF16)

    q0 = qb * tq
    kd = q0 // ck
    nkc = kd + 1
    w = small_ref[:, SMALL_WI_OFF:SMALL_WI_OFF + IDX_HEADS] * idx_scale

    def score_body(kc, carry):
        kt = ki_ref[pl.ds(pl.multiple_of(kc * ck, ck), ck), :]
        lg = lax.dot_general(qis_ref[...], kt, NT_DIMS, preferred_element_type=F32)
        sc = jnp.zeros((tq, ck), F32)
        for h in range(nh):
            sc = sc + w[:, h:h + 1] * jnp.maximum(lg[h * tq:(h + 1) * tq, :], 0.0)
        sc_ref[kc] = sc
        return carry

    lax.fori_loop(0, nkc, score_body, 0)
    qpos = q0 + lax.broadcasted_iota(jnp.int32, (tq, ck), 0)
    kpos = kd * ck + lax.broadcasted_iota(jnp.int32, (tq, ck), 1)
    sc_ref[kd] = jnp.where(kpos <= qpos, sc_ref[kd], -jnp.inf)

    def count_ge(c):
        def body(kc, acc):
            for j in range(ck // LANES):
                x = sc_ref[kc, :, j * LANES:(j + 1) * LANES]
                acc = acc + jnp.where(x >= c, 1.0, 0.0)
            return acc
        acc = lax.fori_loop(0, nkc, body, jnp.zeros((tq, LANES), F32))
        return jnp.sum(acc, axis=1, keepdims=True)

    def bit_body(i, u):
        trial = u | lax.shift_left(jnp.int32(1), 31 - i)
        cnt = count_ge(_key_to_f32(trial))
        return jnp.where(cnt >= float(top_k), trial, u)

    u = lax.fori_loop(0, 32, bit_body, jnp.zeros((tq, 1), jnp.int32))
    qcol = q0 + lax.broadcasted_iota(jnp.int32, (tq, 1), 0)
    thr = jnp.where(qcol < top_k, jnp.finfo(F32).min, _key_to_f32(u))

    m_ref[...] = jnp.full(m_ref.shape, jnp.finfo(F32).min, F32)
    acc_ref[...] = jnp.zeros(acc_ref.shape, F32)

    def attn_body(kc, carry):
        off = pl.multiple_of(kc * ck, ck)
        s = lax.dot_general(qas_ref[...], ka_ref[pl.ds(off, ck), :], NT_DIMS,
                            preferred_element_type=F32)
        neg = jnp.where(sc_ref[kc] >= thr, 0.0, -jnp.inf)
        for h in range(nh):
            rows = slice(h * tq, (h + 1) * tq)
            sh = s[rows, :] + neg
            m_old = m_ref[rows, :]
            m_new = jnp.maximum(m_old, jnp.max(sh, axis=1, keepdims=True))
            alpha_ref[rows, :] = jnp.exp(m_old - m_new)
            m_ref[rows, :] = m_new
            p_ref[rows, :] = jnp.exp(sh - m_new).astype(BF16)
        pv = jnp.dot(p_ref[...], vext_ref[pl.ds(off, ck), :], preferred_element_type=F32)
        acc_ref[...] = alpha_ref[...] * acc_ref[...] + pv
        return carry

    lax.fori_loop(0, nkc, attn_body, 0)

    for h in range(nh):
        a = acc_ref[h * tq:(h + 1) * tq, :]
        cols = slice(h * A_KV_DIM, (h + 1) * A_KV_DIM)
        g = ga_ref[:, cols]
        o_h = a[:, 0:A_KV_DIM] / a[:, A_KV_DIM:A_KV_DIM + 1]
        o_ref[:, cols] = (o_h * (g * _sigmoid(g))).astype(o_ref.dtype)


def _dsa(qi, small, qa, ga, ki, ka, va, B, L):
    tq, ck = DSA_Q_ROWS, min(DSA_KEY_CHUNK, L)
    top_k = min(TOPK_MAX, L // 4)
    nq = L // tq
    row = lambda b, q: (b * nq + q, 0)
    seq = lambda b, q: (b, 0)
    kernel = functools.partial(_dsa_kernel, tq=tq, ck=ck, top_k=top_k)
    return pl.pallas_call(
        kernel,
        grid=(B, nq),
        in_specs=[pl.BlockSpec((tq, A_WIDTH), row),
                  pl.BlockSpec((tq, LANES), row),
                  pl.BlockSpec((tq, A_WIDTH), row),
                  pl.BlockSpec((tq, A_WIDTH), row),
                  pl.BlockSpec((L, IDX_DIM), seq),
                  pl.BlockSpec((L, A_KV_DIM), seq),
                  pl.BlockSpec((L, A_KV_DIM), seq)],
        out_specs=pl.BlockSpec((tq, A_WIDTH), row),
        out_shape=jax.ShapeDtypeStruct((B * L, A_WIDTH), BF16),
        scratch_shapes=[pltpu.VMEM((A_HEADS * tq, IDX_DIM), BF16),
                        pltpu.VMEM((A_HEADS * tq, A_HEAD_DIM), BF16),
                        pltpu.VMEM((L, LANES), BF16),
                        pltpu.VMEM((L // ck, tq, ck), F32),
                        pltpu.VMEM((A_HEADS * tq, 1), F32),
                        pltpu.VMEM((A_HEADS * tq, 1), F32),
                        pltpu.VMEM((A_HEADS * tq, LANES), F32),
                        pltpu.VMEM((A_HEADS * tq, ck), BF16)],
        compiler_params=_params("arbitrary", "arbitrary"),
        name="dsa",
    )(qi, small, qa, ga, ki, ka, va)


def _gla_kernel(q_ref, k_ref, v_ref, gb_ref, small_ref, wup_ref, bg_ref, gh_ref, o_ref, s_ref,
                *, c, sub):
    @pl.when(pl.program_id(2) == 0)
    def _():
        s_ref[...] = jnp.zeros(s_ref.shape, F32)

    hi = lax.Precision.HIGHEST
    q = q_ref[...] * (B_KEY_DIM ** -0.5)
    k = k_ref[...]
    v = v_ref[...]
    xg = jnp.dot(small_ref[...], wup_ref[...], precision=hi, preferred_element_type=F32) + bg_ref[...]
    la = (jnp.minimum(xg, 0.0) - jnp.log1p(jnp.exp(-jnp.abs(xg)))) * (1.0 / GATE_TAU)

    row = lax.broadcasted_iota(jnp.int32, (c, c), 0)
    col = lax.broadcasted_iota(jnp.int32, (c, c), 1)
    tri = jnp.where(col <= row, 1.0, 0.0)
    g = jnp.dot(tri, la, precision=hi, preferred_element_type=F32)
    gt = g.T
    kt = k.T

    o = jnp.dot((q * jnp.exp(g)).astype(BF16), s_ref[...].astype(BF16), preferred_element_type=F32)

    lane = lax.broadcasted_iota(jnp.int32, (B_KEY_DIM, c), 1)
    a_rows = [jnp.zeros((sub, c), F32)]
    for i in range(1, c // sub):
        lo = i * sub
        expo = jnp.where(lane < lo, gt[:, lo - 1:lo] - gt, -jnp.inf)
        kti = (kt * jnp.exp(expo)).astype(BF16)
        qi = (q[lo:lo + sub, :] * jnp.exp(g[lo:lo + sub, :] - g[lo - 1:lo, :])).astype(BF16)
        a_rows.append(jnp.dot(qi, kti, preferred_element_type=F32))
    a = jnp.concatenate(a_rows, axis=0)

    rmod = lax.broadcasted_iota(jnp.int32, (c, 1), 0) % sub
    diff = col - row
    for d in range(sub):
        ks = k if d == 0 else pltpu.roll(k, d, axis=0)
        gs = g if d == 0 else pltpu.roll(g, d, axis=0)
        e = jnp.where(rmod >= d, g - gs, -jnp.inf)
        r = jnp.sum(q * ks * jnp.exp(e), axis=1, keepdims=True)
        a = a + jnp.where(diff == -d, r, 0.0)

    o = o + jnp.dot(a.astype(BF16), v, preferred_element_type=F32)

    glast = gt[:, c - 1:c]
    kend = (kt * jnp.exp(glast - gt)).astype(BF16)
    s_ref[...] = jnp.exp(glast) * s_ref[...] + jnp.dot(kend, v, preferred_element_type=F32)

    gb = gb_ref[...]
    o_ref[...] = (_rms(o, gh_ref[...]) * (gb * _sigmoid(gb))).astype(o_ref.dtype)


def _gla(qb, kb, vb, gb, small, wup_pad, b_gate, g_head, B, L):
    c = GLA_CHUNK
    nc = L // c
    row = lambda b, h, i: (b * nc + i, h)
    return pl.pallas_call(
        functools.partial(_gla_kernel, c=c, sub=GLA_SUB),
        grid=(B, B_HEADS, nc),
        in_specs=[pl.BlockSpec((c, B_KEY_DIM), row),
                  pl.BlockSpec((c, B_KEY_DIM), row),
                  pl.BlockSpec((c, B_VAL_DIM), row),
                  pl.BlockSpec((c, B_VAL_DIM), row),
                  pl.BlockSpec((c, LANES), lambda b, h, i: (b * nc + i, 0)),
                  pl.BlockSpec((LANES, B_KEY_DIM), lambda b, h, i: (0, h)),
                  pl.BlockSpec((1, B_KEY_DIM), lambda b, h, i: (0, h)),
                  pl.BlockSpec((1, B_VAL_DIM), lambda b, h, i: (0, 0))],
        out_specs=pl.BlockSpec((c, B_VAL_DIM), row),
        out_shape=jax.ShapeDtypeStruct((B * L, B_VAL_WIDTH), BF16),
        scratch_shapes=[pltpu.VMEM((B_KEY_DIM, B_VAL_DIM), F32)],
        compiler_params=_params("arbitrary", "arbitrary", "arbitrary"),
        name="gla",
    )(qb, kb, vb, gb, small, wup_pad, b_gate.reshape(1, B_KEY_WIDTH), g_head.reshape(1, B_VAL_DIM))


def _merge_kernel(x_ref, p_ref, oa_ref, ob_ref, ma_ref, mb_ref, wpa_ref, wpb_ref, wout_ref,
                  gpost_ref, wple_ref, wpg_ref, gpre_ref, gpost2_ref, o_ref):
    ya = jnp.dot(oa_ref[...], wpa_ref[...], preferred_element_type=F32)
    yb = jnp.dot(ob_ref[...], wpb_ref[...], preferred_element_type=F32)
    y = _sigmoid(ma_ref[...]) * ya + _sigmoid(mb_ref[...]) * yb
    u = jnp.dot(y.astype(BF16), wout_ref[...], preferred_element_type=F32)
    x1 = x_ref[...] + _rms(u, gpost_ref[...])
    gate = jnp.dot(_rms(x1, gpre_ref[...]).astype(BF16), wpg_ref[...], preferred_element_type=F32)
    e = jnp.dot(p_ref[...].astype(BF16), wple_ref[...], preferred_element_type=F32) * _sigmoid(gate)
    o_ref[...] = x1 + _rms(e, gpost2_ref[...])


def _merge(x2d, p2d, oa, ob, ma, mb, wpa, wpb, wout, g_post, wple, wpg, g_pre2, g_post2):
    T = x2d.shape[0]
    tm = MERGE_ROWS
    rows = lambda width: pl.BlockSpec((tm, width), lambda i: (i, 0))
    full = lambda a: pl.BlockSpec(a.shape, lambda i: (0, 0))
    vec = lambda a: a.reshape(1, D_MODEL)
    args = (x2d, p2d, oa, ob, ma, mb, wpa, wpb, wout, vec(g_post), wple, wpg, vec(g_pre2), vec(g_post2))
    in_specs = [rows(D_MODEL), rows(PLE_DIM), rows(A_WIDTH), rows(B_VAL_WIDTH), rows(D_MODEL), rows(D_MODEL)]
    in_specs += [full(a) for a in args[6:]]
    return pl.pallas_call(
        _merge_kernel,
        grid=(T // tm,),
        in_specs=in_specs,
        out_specs=rows(D_MODEL),
        out_shape=jax.ShapeDtypeStruct((T, D_MODEL), F32),
        compiler_params=_params("arbitrary"),
        name="merge",
    )(*args)


def _split_w_in(w_in):
    offs = [0]
    for s in IN_SPLITS:
        offs.append(offs[-1] + s)
    col = lambda i: w_in[:, offs[i]:offs[i + 1]]
    qa, ka, va, qi, ki, wi, ga, qb, kb, vb, gdown, gb, ma, mb = [col(i) for i in range(len(IN_SPLITS))]
    pad = jnp.zeros((D_MODEL, LANES - IDX_HEADS - GATE_RANK), w_in.dtype)
    small = jnp.concatenate([wi, gdown, pad], axis=1)
    groups = [qa, ka, va, qi, ki, small, ga, qb, kb, vb, gb, ma, mb]
    dtypes = [BF16, BF16, BF16, BF16, BF16, F32, F32, F32, F32, BF16, F32, F32, F32]
    return [g.astype(BF16) for g in groups], dtypes


def _layer(x2d, p2d, B, L, g_pre, w_in, w_gate_up, b_gate, g_gla_head, w_proj_a, w_proj_b,
           w_out, g_post, w_ple, w_ple_gate, g_ple_pre, g_ple_post):
    groups, dtypes = _split_w_in(w_in)
    qa, ka, va, qi, ki, small, ga, qb, kb, vb, gb, ma, mb = _in_proj(x2d, g_pre, groups, dtypes)
    oa = _dsa(qi, small, qa, ga, ki, ka, va, B, L)
    wup_pad = jnp.zeros((LANES, B_KEY_WIDTH), F32).at[SMALL_GD_OFF:SMALL_GD_OFF + GATE_RANK].set(w_gate_up)
    ob = _gla(qb, kb, vb, gb, small, wup_pad, b_gate, g_gla_head, B, L)
    bf = lambda a: a.astype(BF16)
    return _merge(x2d, p2d, oa, ob, ma, mb, bf(w_proj_a), bf(w_proj_b), bf(w_out), g_post,
                  bf(w_ple), bf(w_ple_gate), g_ple_pre, g_ple_post)


def kernel(x, p, g_pre, w_in, w_gate_up, b_gate, g_gla_head, w_proj_a, w_proj_b, w_out, g_post,
           w_ple, w_ple_gate, g_ple_pre, g_ple_post):
    B, L, _ = x.shape
    depth = p.shape[0]
    x2d = x.reshape(B * L, D_MODEL)
    for i in range(depth):
        x2d = _layer(x2d, p[i].reshape(B * L, PLE_DIM), B, L, g_pre[i], w_in[i], w_gate_up[i],
                     b_gate[i], g_gla_head[i], w_proj_a[i], w_proj_b[i], w_out[i], g_post[i],
                     w_ple[i], w_ple_gate[i], g_ple_pre[i], g_ple_post[i])
    return x2d.reshape(B, L, D_MODEL)
```

```python
import functools

import jax
import jax.numpy as jnp
from jax import lax
from jax.experimental import pallas as pl
from jax.experimental.pallas import tpu as pltpu

F32 = jnp.float32
BF16 = jnp.bfloat16

D_MODEL = 1024
PLE_DIM = 256
A_HEADS = 8
A_HEAD_DIM = 64
A_WIDTH = A_HEADS * A_HEAD_DIM
A_KV_DIM = 64
IDX_HEADS = 8
IDX_DIM = 64
TOPK_MAX = 256
B_HEADS = 4
B_KEY_DIM = 128
B_VAL_DIM = 256
B_KEY_WIDTH = B_HEADS * B_KEY_DIM
B_VAL_WIDTH = B_HEADS * B_VAL_DIM
GATE_RANK = 16
GATE_TAU = 16.0
EPS = 1e-6

IN_SPLITS = (A_WIDTH, A_KV_DIM, A_KV_DIM, IDX_HEADS * IDX_DIM, IDX_DIM, IDX_HEADS, A_WIDTH,
             B_KEY_WIDTH, B_KEY_WIDTH, B_VAL_WIDTH, GATE_RANK, B_VAL_WIDTH,
             D_MODEL, D_MODEL)

VMEM_LIMIT_BYTES = 56 * 1024 * 1024
LANES = 128
BF16_SUBLANES = 16

SMALL_GD_OFF = 0

IN_PROJ_ROWS = 256
MERGE_ROWS = 256
DSA_Q_COLS = 128
DSA_KEY_CHUNK = 512
DSA_VEXT_ROWS = A_KV_DIM + BF16_SUBLANES
GLA_CHUNK = 128
GLA_SUB = 16

NT_DIMS = (((1,), (1,)), ((), ()))


def _sigmoid(x):
    return 1.0 / (1.0 + jnp.exp(-x))


def _rms(x, g):
    ms = jnp.mean(x * x, axis=-1, keepdims=True)
    return x * lax.rsqrt(ms + EPS) * g


def _params(*sem):
    return pltpu.CompilerParams(dimension_semantics=sem, vmem_limit_bytes=VMEM_LIMIT_BYTES)


def _in_proj_kernel(x_ref, g_ref, *refs, transposed):
    n_out = len(transposed)
    w_refs, o_refs = refs[:n_out], refs[n_out:]
    h = _rms(x_ref[...], g_ref[...]).astype(BF16)
    for w_ref, o_ref, tr in zip(w_refs, o_refs, transposed):
        if tr:
            z = lax.dot_general(w_ref[...], h, NT_DIMS, preferred_element_type=F32)
        else:
            z = jnp.dot(h, w_ref[...], preferred_element_type=F32)
        o_ref[...] = z.astype(o_ref.dtype)


def _in_proj(x2d, g_pre, w_groups, out_dtypes, transposed):
    T = x2d.shape[0]
    tm = IN_PROJ_ROWS
    in_specs = [pl.BlockSpec((tm, D_MODEL), lambda i: (i, 0)),
                pl.BlockSpec((1, D_MODEL), lambda i: (0, 0))]
    in_specs += [pl.BlockSpec(w.shape, lambda i: (0, 0)) for w in w_groups]
    out_specs, out_shape = [], []
    for w, dt, tr in zip(w_groups, out_dtypes, transposed):
        if tr:
            out_specs.append(pl.BlockSpec((w.shape[0], tm), lambda i: (0, i)))
            out_shape.append(jax.ShapeDtypeStruct((w.shape[0], T), dt))
        else:
            out_specs.append(pl.BlockSpec((tm, w.shape[1]), lambda i: (i, 0)))
            out_shape.append(jax.ShapeDtypeStruct((T, w.shape[1]), dt))
    return pl.pallas_call(
        functools.partial(_in_proj_kernel, transposed=tuple(transposed)),
        grid=(T // tm,),
        in_specs=in_specs,
        out_specs=out_specs,
        out_shape=out_shape,
        compiler_params=_params("arbitrary"),
        name="in_proj",
    )(x2d, g_pre.reshape(1, D_MODEL), *w_groups)


def _key_to_f32(u):
    key = u ^ jnp.int32(-2 ** 31)
    bits = jnp.where(key >= 0, key, key ^ jnp.int32(0x7FFFFFFF))
    return lax.bitcast_convert_type(bits, F32)


def _dsa_kernel(qit_ref, wit_ref, qat_ref, ga_ref, ki_ref, ka_ref, vat_ref, o_ref,
                qis_ref, qas_ref, vext_ref, sc_ref, m_ref, alpha_ref, acc_ref, p_ref,
                *, tq, ck, top_k):
    qb = pl.program_id(1)
    nh = A_HEADS
    idx_scale = (IDX_HEADS ** -0.5) * (IDX_DIM ** -0.5)
    attn_scale = A_HEAD_DIM ** -0.5
    n_chunks = vext_ref.shape[0]

    @pl.when(qb == 0)
    def _():
        for j in range(n_chunks):
            vext_ref[j, 0:A_KV_DIM, :] = vat_ref[:, j * ck:(j + 1) * ck]
            vext_ref[j, A_KV_DIM:, :] = jnp.ones((DSA_VEXT_ROWS - A_KV_DIM, ck), BF16)

    for h in range(nh):
        qis_ref[:, h * tq:(h + 1) * tq] = qit_ref[h * IDX_DIM:(h + 1) * IDX_DIM, :]
        qa_h = qat_ref[h * A_HEAD_DIM:(h + 1) * A_HEAD_DIM, :].astype(F32) * attn_scale
        qas_ref[:, h * tq:(h + 1) * tq] = qa_h.astype(BF16)

    q0 = qb * tq
    kd = q0 // ck
    nkc = kd + 1
    w = wit_ref[...] * idx_scale

    def score_body(kc, carry):
        kt = ki_ref[pl.ds(pl.multiple_of(kc * ck, ck), ck), :]
        lg = jnp.dot(kt, qis_ref[...], preferred_element_type=F32)
        sc = jnp.zeros((ck, tq), F32)
        for h in range(nh):
            sc = sc + w[h:h + 1, :] * jnp.maximum(lg[:, h * tq:(h + 1) * tq], 0.0)
        sc_ref[kc] = sc
        return carry

    lax.fori_loop(0, nkc, score_body, 0)
    kpos = kd * ck + lax.broadcasted_iota(jnp.int32, (ck, tq), 0)
    qpos = q0 + lax.broadcasted_iota(jnp.int32, (ck, tq), 1)
    sc_ref[kd] = jnp.where(kpos <= qpos, sc_ref[kd], -jnp.inf)

    acc_rows = 4 * 8

    def count_ge(c):
        def body(kc, acc):
            hit = jnp.where(sc_ref[kc] >= c, 1.0, 0.0)
            return acc + jnp.sum(hit.reshape(ck // acc_rows, acc_rows, tq), axis=0)
        acc = lax.fori_loop(0, nkc, body, jnp.zeros((acc_rows, tq), F32))
        return jnp.sum(acc, axis=0, keepdims=True)

    def bit_body(i, u):
        trial = u | lax.shift_left(jnp.int32(1), 31 - i)
        cnt = count_ge(_key_to_f32(trial))
        return jnp.where(cnt >= float(top_k), trial, u)

    u = lax.fori_loop(0, 32, bit_body, jnp.zeros((1, tq), jnp.int32))
    qrow = q0 + lax.broadcasted_iota(jnp.int32, (1, tq), 1)
    thr = jnp.where(qrow < top_k, jnp.finfo(F32).min, _key_to_f32(u))

    m_ref[...] = jnp.full(m_ref.shape, jnp.finfo(F32).min, F32)
    acc_ref[...] = jnp.zeros(acc_ref.shape, F32)

    def attn_body(kc, carry):
        off = pl.multiple_of(kc * ck, ck)
        s = jnp.dot(ka_ref[pl.ds(off, ck), :], qas_ref[...], preferred_element_type=F32)
        neg = jnp.where(sc_ref[kc] >= thr, 0.0, -jnp.inf)
        for h in range(nh):
            cols = slice(h * tq, (h + 1) * tq)
            sh = s[:, cols] + neg
            m_old = m_ref[h:h + 1, :]
            m_new = jnp.maximum(m_old, jnp.max(sh, axis=0, keepdims=True))
            alpha_ref[h:h + 1, :] = jnp.exp(m_old - m_new)
            m_ref[h:h + 1, :] = m_new
            p_ref[:, cols] = jnp.exp(sh - m_new).astype(BF16)
        pv = jnp.dot(vext_ref[kc], p_ref[...], preferred_element_type=F32)
        for h in range(nh):
            acc_ref[h] = alpha_ref[h:h + 1, :] * acc_ref[h] + pv[:, h * tq:(h + 1) * tq]
        return carry

    lax.fori_loop(0, nkc, attn_body, 0)

    o_t = []
    for h in range(nh):
        a = acc_ref[h]
        o_t.append(a[0:A_KV_DIM, :] / a[A_KV_DIM:A_KV_DIM + 1, :])
    o = jnp.concatenate(o_t, axis=0).T
    g = ga_ref[...]
    o_ref[...] = (o * (g * _sigmoid(g))).astype(o_ref.dtype)


def _dsa(qit, wit, qat, ga, ki, ka, vat, B, L):
    tq, ck = DSA_Q_COLS, min(DSA_KEY_CHUNK, L)
    top_k = min(TOPK_MAX, L // 4)
    nq = L // tq
    col = lambda b, q: (0, b * nq + q)
    kernel = functools.partial(_dsa_kernel, tq=tq, ck=ck, top_k=top_k)
    return pl.pallas_call(
        kernel,
        grid=(B, nq),
        in_specs=[pl.BlockSpec((A_WIDTH, tq), col),
                  pl.BlockSpec((IDX_HEADS, tq), col),
                  pl.BlockSpec((A_WIDTH, tq), col),
                  pl.BlockSpec((tq, A_WIDTH), lambda b, q: (b * nq + q, 0)),
                  pl.BlockSpec((L, IDX_DIM), lambda b, q: (b, 0)),
                  pl.BlockSpec((L, A_KV_DIM), lambda b, q: (b, 0)),
                  pl.BlockSpec((A_KV_DIM, L), lambda b, q: (0, b))],
        out_specs=pl.BlockSpec((tq, A_WIDTH), lambda b, q: (b * nq + q, 0)),
        out_shape=jax.ShapeDtypeStruct((B * L, A_WIDTH), BF16),
        scratch_shapes=[pltpu.VMEM((IDX_DIM, A_HEADS * tq), BF16),
                        pltpu.VMEM((A_HEAD_DIM, A_HEADS * tq), BF16),
                        pltpu.VMEM((L // ck, DSA_VEXT_ROWS, ck), BF16),
                        pltpu.VMEM((L // ck, ck, tq), F32),
                        pltpu.VMEM((A_HEADS, tq), F32),
                        pltpu.VMEM((A_HEADS, tq), F32),
                        pltpu.VMEM((A_HEADS, DSA_VEXT_ROWS, tq), F32),
                        pltpu.VMEM((ck, A_HEADS * tq), BF16)],
        compiler_params=_params("arbitrary", "arbitrary"),
        name="dsa",
    )(qit, wit, qat, ga, ki, ka, vat)


def _gla_kernel(q_ref, k_ref, v_ref, gb_ref, small_ref, wup_ref, bg_ref, gh_ref, o_ref, s_ref,
                *, c, sub):
    @pl.when(pl.program_id(2) == 0)
    def _():
        s_ref[...] = jnp.zeros(s_ref.shape, F32)

    hi = lax.Precision.HIGHEST
    q = q_ref[...] * (B_KEY_DIM ** -0.5)
    k = k_ref[...]
    v = v_ref[...]
    xg = jnp.dot(small_ref[...], wup_ref[...], precision=hi, preferred_element_type=F32) + bg_ref[...]
    la = (jnp.minimum(xg, 0.0) - jnp.log1p(jnp.exp(-jnp.abs(xg)))) * (1.0 / GATE_TAU)

    row = lax.broadcasted_iota(jnp.int32, (c, c), 0)
    col = lax.broadcasted_iota(jnp.int32, (c, c), 1)
    tri = jnp.where(col <= row, 1.0, 0.0)
    g = jnp.dot(tri, la, precision=hi, preferred_element_type=F32)
    gt = g.T
    kt = k.T

    o = jnp.dot((q * jnp.exp(g)).astype(BF16), s_ref[...].astype(BF16), preferred_element_type=F32)

    lane = lax.broadcasted_iota(jnp.int32, (B_KEY_DIM, c), 1)
    a_rows = [jnp.zeros((sub, c), F32)]
    for i in range(1, c // sub):
        lo = i * sub
        expo = jnp.where(lane < lo, gt[:, lo - 1:lo] - gt, -jnp.inf)
        kti = (kt * jnp.exp(expo)).astype(BF16)
        qi = (q[lo:lo + sub, :] * jnp.exp(g[lo:lo + sub, :] - g[lo - 1:lo, :])).astype(BF16)
        a_rows.append(jnp.dot(qi, kti, preferred_element_type=F32))
    a = jnp.concatenate(a_rows, axis=0)

    rmod = lax.broadcasted_iota(jnp.int32, (c, 1), 0) % sub
    diff = col - row
    for d in range(sub):
        ks = k if d == 0 else pltpu.roll(k, d, axis=0)
        gs = g if d == 0 else pltpu.roll(g, d, axis=0)
        e = jnp.where(rmod >= d, g - gs, -jnp.inf)
        r = jnp.sum(q * ks * jnp.exp(e), axis=1, keepdims=True)
        a = a + jnp.where(diff == -d, r, 0.0)

    o = o + jnp.dot(a.astype(BF16), v, preferred_element_type=F32)

    glast = gt[:, c - 1:c]
    kend = (kt * jnp.exp(glast - gt)).astype(BF16)
    s_ref[...] = jnp.exp(glast) * s_ref[...] + jnp.dot(kend, v, preferred_element_type=F32)

    gb = gb_ref[...]
    o_ref[...] = (_rms(o, gh_ref[...]) * (gb * _sigmoid(gb))).astype(o_ref.dtype)


def _gla(qb, kb, vb, gb, small, wup_pad, b_gate, g_head, B, L):
    c = GLA_CHUNK
    nc = L // c
    row = lambda b, h, i: (b * nc + i, h)
    return pl.pallas_call(
        functools.partial(_gla_kernel, c=c, sub=GLA_SUB),
        grid=(B, B_HEADS, nc),
        in_specs=[pl.BlockSpec((c, B_KEY_DIM), row),
                  pl.BlockSpec((c, B_KEY_DIM), row),
                  pl.BlockSpec((c, B_VAL_DIM), row),
                  pl.BlockSpec((c, B_VAL_DIM), row),
                  pl.BlockSpec((c, LANES), lambda b, h, i: (b * nc + i, 0)),
                  pl.BlockSpec((LANES, B_KEY_DIM), lambda b, h, i: (0, h)),
                  pl.BlockSpec((1, B_KEY_DIM), lambda b, h, i: (0, h)),
                  pl.BlockSpec((1, B_VAL_DIM), lambda b, h, i: (0, 0))],
        out_specs=pl.BlockSpec((c, B_VAL_DIM), row),
        out_shape=jax.ShapeDtypeStruct((B * L, B_VAL_WIDTH), BF16),
        scratch_shapes=[pltpu.VMEM((B_KEY_DIM, B_VAL_DIM), F32)],
        compiler_params=_params("arbitrary", "arbitrary", "arbitrary"),
        name="gla",
    )(qb, kb, vb, gb, small, wup_pad, b_gate.reshape(1, B_KEY_WIDTH), g_head.reshape(1, B_VAL_DIM))


def _merge_kernel(x_ref, p_ref, oa_ref, ob_ref, ma_ref, mb_ref, wpa_ref, wpb_ref, wout_ref,
                  gpost_ref, wple_ref, wpg_ref, gpre_ref, gpost2_ref, o_ref):
    ya = jnp.dot(oa_ref[...], wpa_ref[...], preferred_element_type=F32)
    yb = jnp.dot(ob_ref[...], wpb_ref[...], preferred_element_type=F32)
    y = _sigmoid(ma_ref[...]) * ya + _sigmoid(mb_ref[...]) * yb
    u = jnp.dot(y.astype(BF16), wout_ref[...], preferred_element_type=F32)
    x1 = x_ref[...] + _rms(u, gpost_ref[...])
    gate = jnp.dot(_rms(x1, gpre_ref[...]).astype(BF16), wpg_ref[...], preferred_element_type=F32)
    e = jnp.dot(p_ref[...].astype(BF16), wple_ref[...], preferred_element_type=F32) * _sigmoid(gate)
    o_ref[...] = x1 + _rms(e, gpost2_ref[...])


def _merge(x2d, p2d, oa, ob, ma, mb, wpa, wpb, wout, g_post, wple, wpg, g_pre2, g_post2):
    T = x2d.shape[0]
    tm = MERGE_ROWS
    rows = lambda width: pl.BlockSpec((tm, width), lambda i: (i, 0))
    full = lambda a: pl.BlockSpec(a.shape, lambda i: (0, 0))
    vec = lambda a: a.reshape(1, D_MODEL)
    args = (x2d, p2d, oa, ob, ma, mb, wpa, wpb, wout, vec(g_post), wple, wpg, vec(g_pre2), vec(g_post2))
    in_specs = [rows(D_MODEL), rows(PLE_DIM), rows(A_WIDTH), rows(B_VAL_WIDTH), rows(D_MODEL), rows(D_MODEL)]
    in_specs += [full(a) for a in args[6:]]
    return pl.pallas_call(
        _merge_kernel,
        grid=(T // tm,),
        in_specs=in_specs,
        out_specs=rows(D_MODEL),
        out_shape=jax.ShapeDtypeStruct((T, D_MODEL), F32),
        compiler_params=_params("arbitrary"),
        name="merge",
    )(*args)


def _split_w_in(w_in):
    offs = [0]
    for s in IN_SPLITS:
        offs.append(offs[-1] + s)
    col = lambda i: w_in[:, offs[i]:offs[i + 1]]
    qa, ka, va, qi, ki, wi, ga, qb, kb, vb, gdown, gb, ma, mb = [col(i) for i in range(len(IN_SPLITS))]
    pad = jnp.zeros((D_MODEL, LANES - GATE_RANK), w_in.dtype)
    small = jnp.concatenate([gdown, pad], axis=1)
    groups = [qa.T, ka, va.T, qi.T, ki, wi.T, small, ga, qb, kb, vb, gb, ma, mb]
    dtypes = [BF16, BF16, BF16, BF16, BF16, F32, F32, F32, F32, F32, BF16, F32, F32, F32]
    transposed = [True, False, True, True, False, True] + [False] * 8
    return [g.astype(BF16) for g in groups], dtypes, transposed


def _layer(x2d, p2d, B, L, g_pre, w_in, w_gate_up, b_gate, g_gla_head, w_proj_a, w_proj_b,
           w_out, g_post, w_ple, w_ple_gate, g_ple_pre, g_ple_post):
    groups, dtypes, transposed = _split_w_in(w_in)
    qat, ka, vat, qit, ki, wit, small, ga, qb, kb, vb, gb, ma, mb = _in_proj(
        x2d, g_pre, groups, dtypes, transposed)
    oa = _dsa(qit, wit, qat, ga, ki, ka, vat, B, L)
    wup_pad = jnp.zeros((LANES, B_KEY_WIDTH), F32).at[SMALL_GD_OFF:SMALL_GD_OFF + GATE_RANK].set(w_gate_up)
    ob = _gla(qb, kb, vb, gb, small, wup_pad, b_gate, g_gla_head, B, L)
    bf = lambda a: a.astype(BF16)
    return _merge(x2d, p2d, oa, ob, ma, mb, bf(w_proj_a), bf(w_proj_b), bf(w_out), g_post,
                  bf(w_ple), bf(w_ple_gate), g_ple_pre, g_ple_post)


def kernel(x, p, g_pre, w_in, w_gate_up, b_gate, g_gla_head, w_proj_a, w_proj_b, w_out, g_post,
           w_ple, w_ple_gate, g_ple_pre, g_ple_post):
    B, L, _ = x.shape
    depth = p.shape[0]
    x2d = x.reshape(B * L, D_MODEL)
    for i in range(depth):
        x2d = _layer(x2d, p[i].reshape(B * L, PLE_DIM), B, L, g_pre[i], w_in[i], w_gate_up[i],
                     b_gate[i], g_gla_head[i], w_proj_a[i], w_proj_b[i], w_out[i], g_post[i],
                     w_ple[i], w_ple_gate[i], g_ple_pre[i], g_ple_post[i])
    return x2d.reshape(B, L, D_MODEL)
```

```python
import functools

import jax
import jax.numpy as jnp
from jax import lax
from jax.experimental import pallas as pl
from jax.experimental.pallas import tpu as pltpu

F32 = jnp.float32
BF16 = jnp.bfloat16

D_MODEL = 1024
PLE_DIM = 256
A_HEADS = 8
A_HEAD_DIM = 64
A_WIDTH = A_HEADS * A_HEAD_DIM
A_KV_DIM = 64
IDX_HEADS = 8
IDX_DIM = 64
TOPK_MAX = 256
B_HEADS = 4
B_KEY_DIM = 128
B_VAL_DIM = 256
B_KEY_WIDTH = B_HEADS * B_KEY_DIM
B_VAL_WIDTH = B_HEADS * B_VAL_DIM
GATE_RANK = 16
GATE_TAU = 16.0
EPS = 1e-6

IN_SPLITS = (A_WIDTH, A_KV_DIM, A_KV_DIM, IDX_HEADS * IDX_DIM, IDX_DIM, IDX_HEADS, A_WIDTH,
             B_KEY_WIDTH, B_KEY_WIDTH, B_VAL_WIDTH, GATE_RANK, B_VAL_WIDTH,
             D_MODEL, D_MODEL)

VMEM_LIMIT_BYTES = 56 * 1024 * 1024
LANES = 128
BF16_SUBLANES = 16

SMALL_GD_OFF = 0

IN_PROJ_ROWS = 256
MERGE_ROWS = 256
DSA_Q_COLS = 128
DSA_KEY_CHUNK = 512
DSA_VEXT_ROWS = A_KV_DIM + BF16_SUBLANES
GLA_CHUNK = 128
GLA_SUB = 16

NT_DIMS = (((1,), (1,)), ((), ()))


def _sigmoid(x):
    return 1.0 / (1.0 + jnp.exp(-x))


def _rms(x, g):
    ms = jnp.mean(x * x, axis=-1, keepdims=True)
    return x * lax.rsqrt(ms + EPS) * g


def _params(*sem):
    return pltpu.CompilerParams(dimension_semantics=sem, vmem_limit_bytes=VMEM_LIMIT_BYTES)


def _in_proj_kernel(x_ref, g_ref, *refs, transposed):
    n_out = len(transposed)
    w_refs, o_refs = refs[:n_out], refs[n_out:]
    h = _rms(x_ref[...], g_ref[...]).astype(BF16)
    for w_ref, o_ref, tr in zip(w_refs, o_refs, transposed):
        if tr:
            z = lax.dot_general(w_ref[...], h, NT_DIMS, preferred_element_type=F32)
        else:
            z = jnp.dot(h, w_ref[...], preferred_element_type=F32)
        o_ref[...] = z.astype(o_ref.dtype)


def _in_proj(x2d, g_pre, w_groups, out_dtypes, transposed):
    T = x2d.shape[0]
    tm = IN_PROJ_ROWS
    in_specs = [pl.BlockSpec((tm, D_MODEL), lambda i: (i, 0)),
                pl.BlockSpec((1, D_MODEL), lambda i: (0, 0))]
    in_specs += [pl.BlockSpec(w.shape, lambda i: (0, 0)) for w in w_groups]
    out_specs, out_shape = [], []
    for w, dt, tr in zip(w_groups, out_dtypes, transposed):
        if tr:
            out_specs.append(pl.BlockSpec((w.shape[0], tm), lambda i: (0, i)))
            out_shape.append(jax.ShapeDtypeStruct((w.shape[0], T), dt))
        else:
            out_specs.append(pl.BlockSpec((tm, w.shape[1]), lambda i: (i, 0)))
            out_shape.append(jax.ShapeDtypeStruct((T, w.shape[1]), dt))
    return pl.pallas_call(
        functools.partial(_in_proj_kernel, transposed=tuple(transposed)),
        grid=(T // tm,),
        in_specs=in_specs,
        out_specs=out_specs,
        out_shape=out_shape,
        compiler_params=_params("arbitrary"),
        name="in_proj",
    )(x2d, g_pre.reshape(1, D_MODEL), *w_groups)


INT16_MIN = -32768
LOG2E = 1.4426950408889634
COUNT_ROWS = 4 * BF16_SUBLANES
MAX_ROWS = 4 * 8


def _dsa_kernel(qit_ref, wit_ref, qat_ref, ga_ref, ki_ref, ka_ref, vat_ref, o_ref,
                qis_ref, qas_ref, vext_ref, sc_ref, hi_ref, lo_ref, lom_ref, sa_ref, sb_ref,
                m_ref, alpha_ref, acc_ref, p_ref, *, tq, ck, top_k):
    qb = pl.program_id(1)
    nh = A_HEADS
    idx_scale = (IDX_HEADS ** -0.5) * (IDX_DIM ** -0.5)
    attn_scale = (A_HEAD_DIM ** -0.5) * LOG2E
    n_chunks = vext_ref.shape[0]

    @pl.when(qb == 0)
    def _():
        for j in range(n_chunks):
            vext_ref[j, 0:A_KV_DIM, :] = vat_ref[:, j * ck:(j + 1) * ck]
            vext_ref[j, A_KV_DIM:, :] = jnp.ones((DSA_VEXT_ROWS - A_KV_DIM, ck), BF16)

    for h in range(nh):
        qis_ref[:, h * tq:(h + 1) * tq] = qit_ref[h * IDX_DIM:(h + 1) * IDX_DIM, :]
        qa_h = qat_ref[h * A_HEAD_DIM:(h + 1) * A_HEAD_DIM, :].astype(F32) * attn_scale
        qas_ref[:, h * tq:(h + 1) * tq] = qa_h.astype(BF16)

    q0 = qb * tq
    kd = q0 // ck
    nkc = kd + 1
    w = wit_ref[...] * idx_scale

    def store_planes(kc, sc):
        bits = lax.bitcast_convert_type(sc, jnp.int32)
        key = bits ^ ((bits >> 31) & jnp.int32(0x7FFFFFFF))
        hi_ref[kc] = (key >> 16).astype(jnp.int16)
        lo_ref[kc] = ((key & jnp.int32(0xFFFF)) + INT16_MIN).astype(jnp.int16)

    def score_body(kc, carry):
        kt = ki_ref[pl.ds(pl.multiple_of(kc * ck, ck), ck), :]
        lg = jnp.dot(kt, qis_ref[...], preferred_element_type=F32)
        sc = jnp.zeros((ck, tq), F32)
        for h in range(nh):
            sc = sc + w[h:h + 1, :] * jnp.maximum(lg[:, h * tq:(h + 1) * tq], 0.0)
        sc_ref[kc] = sc
        store_planes(kc, sc)
        return carry

    lax.fori_loop(0, nkc, score_body, 0)
    kpos = kd * ck + lax.broadcasted_iota(jnp.int32, (ck, tq), 0)
    qpos = q0 + lax.broadcasted_iota(jnp.int32, (ck, tq), 1)
    sc_diag = jnp.where(kpos <= qpos, sc_ref[kd], -jnp.inf)
    sc_ref[kd] = sc_diag
    store_planes(kd, sc_diag)

    def count_ge(plane_ref, c):
        def body(kc, acc):
            hit = jnp.where(plane_ref[kc] >= c, jnp.int16(1), jnp.int16(0))
            hit = hit.reshape(ck // COUNT_ROWS, COUNT_ROWS, tq)
            for i in range(ck // COUNT_ROWS):
                acc = acc + hit[i]
            return acc
        acc = lax.fori_loop(0, nkc, body, jnp.zeros((COUNT_ROWS, tq), jnp.int16))
        return jnp.sum(acc.astype(jnp.int32), axis=0, keepdims=True)

    def search16(plane_ref, need):
        def bit_body(i, u):
            trial = u | lax.shift_left(jnp.int32(1), 15 - i)
            cnt = count_ge(plane_ref, (trial + INT16_MIN).astype(jnp.int16))
            return jnp.where(cnt >= need, trial, u)
        return lax.fori_loop(0, 16, bit_body, jnp.zeros((1, tq), jnp.int32))

    hsel = search16(hi_ref, top_k) + INT16_MIN
    above = jnp.minimum(hsel + 1, -INT16_MIN - 1).astype(jnp.int16)
    n_gt = jnp.where(hsel >= -INT16_MIN - 1, 0, count_ge(hi_ref, above))
    h16 = hsel.astype(jnp.int16)

    def member_body(kc, carry):
        lom_ref[kc] = jnp.where(hi_ref[kc] == h16, lo_ref[kc], jnp.int16(INT16_MIN))
        return carry

    lax.fori_loop(0, nkc, member_body, 0)
    lsel = search16(lom_ref, top_k - n_gt)
    key = hsel * 65536 + lsel
    thr = lax.bitcast_convert_type(jnp.where(key >= 0, key, key ^ jnp.int32(0x7FFFFFFF)), F32)
    qrow = q0 + lax.broadcasted_iota(jnp.int32, (1, tq), 1)
    thr = jnp.where(qrow < top_k, jnp.finfo(F32).min, thr)

    m_ref[...] = jnp.full(m_ref.shape, jnp.finfo(F32).min, F32)
    acc_ref[...] = jnp.zeros(acc_ref.shape, F32)

    def issue_logits(kc, s_ref):
        kt = ka_ref[pl.ds(pl.multiple_of(kc * ck, ck), ck), :]
        s_ref[...] = jnp.dot(kt, qas_ref[...], preferred_element_type=F32)

    @pl.when(nkc % 2 == 1)
    def _():
        sc_ref[nkc] = jnp.full((ck, tq), -jnp.inf, F32)

    def attend(kc, s_ref):
        sc_ref[kc] = jnp.where(sc_ref[kc] >= thr, 0.0, -jnp.inf)
        for h in range(nh):
            cols = slice(h * tq, (h + 1) * tq)
            sh = s_ref[:, cols] + sc_ref[kc]
            s_ref[:, cols] = sh
            cm = jnp.max(sh.reshape(ck // MAX_ROWS, MAX_ROWS, tq), axis=0)
            m_old = m_ref[h:h + 1, :]
            m_new = jnp.maximum(m_old, jnp.max(cm, axis=0, keepdims=True))
            alpha_ref[h:h + 1, :] = jnp.exp2(m_old - m_new)
            m_ref[h:h + 1, :] = m_new
        for hp in range(nh // 2):
            for h in (2 * hp, 2 * hp + 1):
                cols = slice(h * tq, (h + 1) * tq)
                p_ref[:, cols] = jnp.exp2(s_ref[:, cols] - m_ref[h:h + 1, :]).astype(BF16)
            pv = jnp.dot(vext_ref[kc], p_ref[:, hp * 2 * tq:(hp + 1) * 2 * tq],
                         preferred_element_type=F32)
            for h in (2 * hp, 2 * hp + 1):
                acc_ref[h] = alpha_ref[h:h + 1, :] * acc_ref[h] + pv[:, (h % 2) * tq:(h % 2 + 1) * tq]

    issue_logits(0, sa_ref)

    def pair_body(j, carry):
        c0 = 2 * j
        issue_logits(c0 + 1, sb_ref)
        attend(c0, sa_ref)
        issue_logits(jnp.minimum(c0 + 2, n_chunks - 1), sa_ref)
        attend(c0 + 1, sb_ref)
        return carry

    lax.fori_loop(0, (nkc + 1) // 2, pair_body, 0)

    o_t = []
    for h in range(nh):
        a = acc_ref[h]
        o_t.append(a[0:A_KV_DIM, :] / a[A_KV_DIM:A_KV_DIM + 1, :])
    o = jnp.concatenate(o_t, axis=0).T
    g = ga_ref[...]
    o_ref[...] = (o * (g * _sigmoid(g))).astype(o_ref.dtype)


def _dsa(qit, wit, qat, ga, ki, ka, vat, B, L):
    tq, ck = DSA_Q_COLS, min(DSA_KEY_CHUNK, L)
    top_k = min(TOPK_MAX, L // 4)
    nq = L // tq
    nck = L // ck
    assert L % ck == 0 and nck % 2 == 0, "dsa walks key chunks in pairs"
    col = lambda b, q: (0, b * nq + q)
    kernel = functools.partial(_dsa_kernel, tq=tq, ck=ck, top_k=top_k)
    return pl.pallas_call(
        kernel,
        grid=(B, nq),
        in_specs=[pl.BlockSpec((A_WIDTH, tq), col),
                  pl.BlockSpec((IDX_HEADS, tq), col),
                  pl.BlockSpec((A_WIDTH, tq), col),
                  pl.BlockSpec((tq, A_WIDTH), lambda b, q: (b * nq + q, 0)),
                  pl.BlockSpec((L, IDX_DIM), lambda b, q: (b, 0)),
                  pl.BlockSpec((L, A_KV_DIM), lambda b, q: (b, 0)),
                  pl.BlockSpec((A_KV_DIM, L), lambda b, q: (0, b))],
        out_specs=pl.BlockSpec((tq, A_WIDTH), lambda b, q: (b * nq + q, 0)),
        out_shape=jax.ShapeDtypeStruct((B * L, A_WIDTH), BF16),
        scratch_shapes=[pltpu.VMEM((IDX_DIM, A_HEADS * tq), BF16),
                        pltpu.VMEM((A_HEAD_DIM, A_HEADS * tq), BF16),
                        pltpu.VMEM((nck, DSA_VEXT_ROWS, ck), BF16),
                        pltpu.VMEM((nck, ck, tq), F32),
                        pltpu.VMEM((nck, ck, tq), jnp.int16),
                        pltpu.VMEM((nck, ck, tq), jnp.int16),
                        pltpu.VMEM((nck, ck, tq), jnp.int16),
                        pltpu.VMEM((ck, A_HEADS * tq), F32),
                        pltpu.VMEM((ck, A_HEADS * tq), F32),
                        pltpu.VMEM((A_HEADS, tq), F32),
                        pltpu.VMEM((A_HEADS, tq), F32),
                        pltpu.VMEM((A_HEADS, DSA_VEXT_ROWS, tq), F32),
                        pltpu.VMEM((ck, A_HEADS * tq), BF16)],
        compiler_params=_params("arbitrary", "arbitrary"),
        name="dsa",
    )(qit, wit, qat, ga, ki, ka, vat)


def _gla_kernel(q_ref, k_ref, v_ref, gb_ref, small_ref, wup_ref, bg_ref, gh_ref, o_ref, s_ref,
                *, c, sub):
    @pl.when(pl.program_id(1) == 0)
    def _():
        s_ref[...] = jnp.zeros(s_ref.shape, F32)

    hi = lax.Precision.HIGHEST
    row = lax.broadcasted_iota(jnp.int32, (c, c), 0)
    col = lax.broadcasted_iota(jnp.int32, (c, c), 1)
    tri = jnp.where(col <= row, 1.0, 0.0)
    lane = lax.broadcasted_iota(jnp.int32, (B_KEY_DIM, c), 1)
    rmod = lax.broadcasted_iota(jnp.int32, (c, B_KEY_DIM), 0) % sub
    diff = col - row
    ones = jnp.ones((B_KEY_DIM, c), BF16)
    small = small_ref[...]

    for h in range(B_HEADS):
        kcols = slice(h * B_KEY_DIM, (h + 1) * B_KEY_DIM)
        vcols = slice(h * B_VAL_DIM, (h + 1) * B_VAL_DIM)
        q = q_ref[:, kcols] * (B_KEY_DIM ** -0.5)
        k = k_ref[:, kcols]
        v = v_ref[:, vcols]
        xg = jnp.dot(small, wup_ref[:, kcols], precision=hi, preferred_element_type=F32) + bg_ref[:, kcols]
        la = (jnp.minimum(xg, 0.0) - jnp.log(1.0 + jnp.exp(-jnp.abs(xg)))) * (1.0 / GATE_TAU)
        g = jnp.dot(tri, la, precision=hi, preferred_element_type=F32)
        gt = g.T
        kt = k.T

        o = jnp.dot((q * jnp.exp(g)).astype(BF16), s_ref[h].astype(BF16), preferred_element_type=F32)

        a_rows = [jnp.zeros((sub, c), F32)]
        for i in range(1, c // sub):
            lo = i * sub
            expo = jnp.where(lane < lo, gt[:, lo - 1:lo] - gt, -jnp.inf)
            kti = (kt * jnp.exp(expo)).astype(BF16)
            qi = (q[lo:lo + sub, :] * jnp.exp(g[lo:lo + sub, :] - g[lo - 1:lo, :])).astype(BF16)
            a_rows.append(jnp.dot(qi, kti, preferred_element_type=F32))
        a = jnp.concatenate(a_rows, axis=0)

        for d in range(sub):
            ks = k if d == 0 else pltpu.roll(k, d, axis=0)
            gs = g if d == 0 else pltpu.roll(g, d, axis=0)
            e = jnp.where(rmod >= d, g - gs, -jnp.inf)
            r = jnp.dot((q * ks * jnp.exp(e)).astype(BF16), ones, preferred_element_type=F32)
            a = a + jnp.where(diff == -d, r, 0.0)

        o = o + jnp.dot(a.astype(BF16), v, preferred_element_type=F32)

        glast = gt[:, c - 1:c]
        kend = (kt * jnp.exp(glast - gt)).astype(BF16)
        s_ref[h] = jnp.exp(glast) * s_ref[h] + jnp.dot(kend, v, preferred_element_type=F32)

        gb = gb_ref[:, vcols]
        o_ref[:, vcols] = (_rms(o, gh_ref[...]) * (gb * _sigmoid(gb))).astype(o_ref.dtype)


def _gla(qb, kb, vb, gb, small, wup_pad, b_gate, g_head, B, L):
    c = GLA_CHUNK
    nc = L // c
    row = lambda b, i: (b * nc + i, 0)
    fixed = lambda b, i: (0, 0)
    return pl.pallas_call(
        functools.partial(_gla_kernel, c=c, sub=GLA_SUB),
        grid=(B, nc),
        in_specs=[pl.BlockSpec((c, B_KEY_WIDTH), row),
                  pl.BlockSpec((c, B_KEY_WIDTH), row),
                  pl.BlockSpec((c, B_VAL_WIDTH), row),
                  pl.BlockSpec((c, B_VAL_WIDTH), row),
                  pl.BlockSpec((c, LANES), row),
                  pl.BlockSpec((LANES, B_KEY_WIDTH), fixed),
                  pl.BlockSpec((1, B_KEY_WIDTH), fixed),
                  pl.BlockSpec((1, B_VAL_DIM), fixed)],
        out_specs=pl.BlockSpec((c, B_VAL_WIDTH), row),
        out_shape=jax.ShapeDtypeStruct((B * L, B_VAL_WIDTH), BF16),
        scratch_shapes=[pltpu.VMEM((B_HEADS, B_KEY_DIM, B_VAL_DIM), F32)],
        compiler_params=_params("arbitrary", "arbitrary"),
        name="gla",
    )(qb, kb, vb, gb, small, wup_pad, b_gate.reshape(1, B_KEY_WIDTH), g_head.reshape(1, B_VAL_DIM))


def _merge_kernel(x_ref, p_ref, oa_ref, ob_ref, ma_ref, mb_ref, wpa_ref, wpb_ref, wout_ref,
                  gpost_ref, wple_ref, wpg_ref, gpre_ref, gpost2_ref, o_ref):
    ya = jnp.dot(oa_ref[...], wpa_ref[...], preferred_element_type=F32)
    yb = jnp.dot(ob_ref[...], wpb_ref[...], preferred_element_type=F32)
    y = _sigmoid(ma_ref[...]) * ya + _sigmoid(mb_ref[...]) * yb
    u = jnp.dot(y.astype(BF16), wout_ref[...], preferred_element_type=F32)
    x1 = x_ref[...] + _rms(u, gpost_ref[...])
    gate = jnp.dot(_rms(x1, gpre_ref[...]).astype(BF16), wpg_ref[...], preferred_element_type=F32)
    e = jnp.dot(p_ref[...].astype(BF16), wple_ref[...], preferred_element_type=F32) * _sigmoid(gate)
    o_ref[...] = x1 + _rms(e, gpost2_ref[...])


def _merge(x2d, p2d, oa, ob, ma, mb, wpa, wpb, wout, g_post, wple, wpg, g_pre2, g_post2):
    T = x2d.shape[0]
    tm = MERGE_ROWS
    rows = lambda width: pl.BlockSpec((tm, width), lambda i: (i, 0))
    full = lambda a: pl.BlockSpec(a.shape, lambda i: (0, 0))
    vec = lambda a: a.reshape(1, D_MODEL)
    args = (x2d, p2d, oa, ob, ma, mb, wpa, wpb, wout, vec(g_post), wple, wpg, vec(g_pre2), vec(g_post2))
    in_specs = [rows(D_MODEL), rows(PLE_DIM), rows(A_WIDTH), rows(B_VAL_WIDTH), rows(D_MODEL), rows(D_MODEL)]
    in_specs += [full(a) for a in args[6:]]
    return pl.pallas_call(
        _merge_kernel,
        grid=(T // tm,),
        in_specs=in_specs,
        out_specs=rows(D_MODEL),
        out_shape=jax.ShapeDtypeStruct((T, D_MODEL), F32),
        compiler_params=_params("arbitrary"),
        name="merge",
    )(*args)


def _split_w_in(w_in):
    offs = [0]
    for s in IN_SPLITS:
        offs.append(offs[-1] + s)
    col = lambda i: w_in[:, offs[i]:offs[i + 1]]
    qa, ka, va, qi, ki, wi, ga, qb, kb, vb, gdown, gb, ma, mb = [col(i) for i in range(len(IN_SPLITS))]
    pad = jnp.zeros((D_MODEL, LANES - GATE_RANK), w_in.dtype)
    small = jnp.concatenate([gdown, pad], axis=1)
    groups = [qa.T, ka, va.T, qi.T, ki, wi.T, small, ga, qb, kb, vb, gb, ma, mb]
    dtypes = [BF16, BF16, BF16, BF16, BF16, F32, F32, F32, F32, F32, BF16, F32, F32, F32]
    transposed = [True, False, True, True, False, True] + [False] * 8
    return [g.astype(BF16) for g in groups], dtypes, transposed


def _layer(x2d, p2d, B, L, g_pre, w_in, w_gate_up, b_gate, g_gla_head, w_proj_a, w_proj_b,
           w_out, g_post, w_ple, w_ple_gate, g_ple_pre, g_ple_post):
    groups, dtypes, transposed = _split_w_in(w_in)
    qat, ka, vat, qit, ki, wit, small, ga, qb, kb, vb, gb, ma, mb = _in_proj(
        x2d, g_pre, groups, dtypes, transposed)
    oa = _dsa(qit, wit, qat, ga, ki, ka, vat, B, L)
    wup_pad = jnp.zeros((LANES, B_KEY_WIDTH), F32).at[SMALL_GD_OFF:SMALL_GD_OFF + GATE_RANK].set(w_gate_up)
    ob = _gla(qb, kb, vb, gb, small, wup_pad, b_gate, g_gla_head, B, L)
    bf = lambda a: a.astype(BF16)
    return _merge(x2d, p2d, oa, ob, ma, mb, bf(w_proj_a), bf(w_proj_b), bf(w_out), g_post,
                  bf(w_ple), bf(w_ple_gate), g_ple_pre, g_ple_post)


def kernel(x, p, g_pre, w_in, w_gate_up, b_gate, g_gla_head, w_proj_a, w_proj_b, w_out, g_post,
           w_ple, w_ple_gate, g_ple_pre, g_ple_post):
    B, L, _ = x.shape
    depth = p.shape[0]
    x2d = x.reshape(B * L, D_MODEL)
    for i in range(depth):
        x2d = _layer(x2d, p[i].reshape(B * L, PLE_DIM), B, L, g_pre[i], w_in[i], w_gate_up[i],
                     b_gate[i], g_gla_head[i], w_proj_a[i], w_proj_b[i], w_out[i], g_post[i],
                     w_ple[i], w_ple_gate[i], g_ple_pre[i], g_ple_post[i])
    return x2d.reshape(B, L, D_MODEL)
```

```python
import functools

import jax
import jax.numpy as jnp
from jax import lax
from jax.experimental import pallas as pl
from jax.experimental.pallas import tpu as pltpu

F32 = jnp.float32
BF16 = jnp.bfloat16

D_MODEL = 1024
PLE_DIM = 256
A_HEADS = 8
A_HEAD_DIM = 64
A_WIDTH = A_HEADS * A_HEAD_DIM
A_KV_DIM = 64
IDX_HEADS = 8
IDX_DIM = 64
TOPK_MAX = 256
B_HEADS = 4
B_KEY_DIM = 128
B_VAL_DIM = 256
B_KEY_WIDTH = B_HEADS * B_KEY_DIM
B_VAL_WIDTH = B_HEADS * B_VAL_DIM
GATE_RANK = 16
GATE_TAU = 16.0
EPS = 1e-6

IN_SPLITS = (A_WIDTH, A_KV_DIM, A_KV_DIM, IDX_HEADS * IDX_DIM, IDX_DIM, IDX_HEADS, A_WIDTH,
             B_KEY_WIDTH, B_KEY_WIDTH, B_VAL_WIDTH, GATE_RANK, B_VAL_WIDTH,
             D_MODEL, D_MODEL)

VMEM_LIMIT_BYTES = 56 * 1024 * 1024
LANES = 128
BF16_SUBLANES = 16

SMALL_GD_OFF = 0

IN_PROJ_ROWS = 256
MERGE_ROWS = 256
DSA_Q_COLS = 128
DSA_KEY_CHUNK = 512
DSA_VEXT_ROWS = A_KV_DIM + BF16_SUBLANES
GLA_CHUNK = 128
GLA_SUB = 16

NT_DIMS = (((1,), (1,)), ((), ()))


def _sigmoid(x):
    return 1.0 / (1.0 + jnp.exp(-x))


def _rms(x, g):
    ms = jnp.mean(x * x, axis=-1, keepdims=True)
    return x * lax.rsqrt(ms + EPS) * g


def _params(*sem):
    return pltpu.CompilerParams(dimension_semantics=sem, vmem_limit_bytes=VMEM_LIMIT_BYTES)


def _in_proj_kernel(x_ref, g_ref, *refs, transposed):
    n_out = len(transposed)
    w_refs, o_refs = refs[:n_out], refs[n_out:]
    h = _rms(x_ref[...], g_ref[...]).astype(BF16)
    for w_ref, o_ref, tr in zip(w_refs, o_refs, transposed):
        if tr:
            z = lax.dot_general(w_ref[...], h, NT_DIMS, preferred_element_type=F32)
        else:
            z = jnp.dot(h, w_ref[...], preferred_element_type=F32)
        o_ref[...] = z.astype(o_ref.dtype)


def _in_proj(x2d, g_pre, w_groups, out_dtypes, transposed):
    T = x2d.shape[0]
    tm = IN_PROJ_ROWS
    in_specs = [pl.BlockSpec((tm, D_MODEL), lambda i: (i, 0)),
                pl.BlockSpec((1, D_MODEL), lambda i: (0, 0))]
    in_specs += [pl.BlockSpec(w.shape, lambda i: (0, 0)) for w in w_groups]
    out_specs, out_shape = [], []
    for w, dt, tr in zip(w_groups, out_dtypes, transposed):
        if tr:
            out_specs.append(pl.BlockSpec((w.shape[0], tm), lambda i: (0, i)))
            out_shape.append(jax.ShapeDtypeStruct((w.shape[0], T), dt))
        else:
            out_specs.append(pl.BlockSpec((tm, w.shape[1]), lambda i: (i, 0)))
            out_shape.append(jax.ShapeDtypeStruct((T, w.shape[1]), dt))
    return pl.pallas_call(
        functools.partial(_in_proj_kernel, transposed=tuple(transposed)),
        grid=(T // tm,),
        in_specs=in_specs,
        out_specs=out_specs,
        out_shape=out_shape,
        compiler_params=_params("arbitrary"),
        name="in_proj",
    )(x2d, g_pre.reshape(1, D_MODEL), *w_groups)


LOG2E = 1.4426950408889634
MAX_ROWS = 4 * 8
F32_SUBLANES = 8
KEY_BITS = 32
BLOCK_KEYS = KEY_BITS * F32_SUBLANES


def _bit_transpose(words):
    a = list(words)
    j, m = 16, 0x0000FFFF
    while j:
        mask = jnp.int32(m if m < 2 ** 31 else m - 2 ** 32)
        k = 0
        while k < KEY_BITS:
            t = (a[k] ^ lax.shift_right_logical(a[k + j], jnp.int32(j))) & mask
            a[k] = a[k] ^ t
            a[k + j] = a[k + j] ^ lax.shift_left(t, jnp.int32(j))
            k = (k + j + 1) & ~j
        j >>= 1
        m = (m ^ (m << j)) & 0xFFFFFFFF
    return a


def _dsa_kernel(qit_ref, wit_ref, qat_ref, ga_ref, ki_ref, ka_ref, vat_ref, o_ref,
                qis_ref, qas_ref, vext_ref, sc_ref, planes_ref, cand_ref,
                m_ref, alpha_ref, acc_ref, p_ref, *, tq, ck, top_k):
    qb = pl.program_id(1)
    nh = A_HEADS
    idx_scale = (IDX_HEADS ** -0.5) * (IDX_DIM ** -0.5)
    attn_scale = (A_HEAD_DIM ** -0.5) * LOG2E
    n_chunks = vext_ref.shape[0]
    n_blocks = cand_ref.shape[0]
    blocks_per_chunk = ck // BLOCK_KEYS

    @pl.when(qb == 0)
    def _():
        for j in range(n_chunks):
            vext_ref[j, 0:A_KV_DIM, :] = vat_ref[:, j * ck:(j + 1) * ck]
            vext_ref[j, A_KV_DIM:, :] = jnp.ones((DSA_VEXT_ROWS - A_KV_DIM, ck), BF16)
        planes_ref[...] = jnp.zeros(planes_ref.shape, jnp.int32)

    for h in range(nh):
        qis_ref[:, h * tq:(h + 1) * tq] = qit_ref[h * IDX_DIM:(h + 1) * IDX_DIM, :]
        qa_h = qat_ref[h * A_HEAD_DIM:(h + 1) * A_HEAD_DIM, :].astype(F32) * attn_scale
        qas_ref[:, h * tq:(h + 1) * tq] = qa_h.astype(BF16)

    q0 = qb * tq
    kd = q0 // ck
    nkc = kd + 1
    w = wit_ref[...] * idx_scale

    def store_planes(kc, sc):
        bits = lax.bitcast_convert_type(sc, jnp.int32)
        ukey = bits ^ ((bits >> 31) | jnp.int32(-2 ** 31))
        for j in range(blocks_per_chunk):
            rows = [ukey[(j * KEY_BITS + e) * F32_SUBLANES:(j * KEY_BITS + e + 1) * F32_SUBLANES, :]
                    for e in range(KEY_BITS)]
            planes = _bit_transpose(rows)
            for b in range(KEY_BITS):
                planes_ref[b, kc * blocks_per_chunk + j] = planes[b]

    def score_body(kc, carry):
        kt = ki_ref[pl.ds(pl.multiple_of(kc * ck, ck), ck), :]
        lg = jnp.dot(kt, qis_ref[...], preferred_element_type=F32)
        sc = jnp.zeros((ck, tq), F32)
        for h in range(nh):
            sc = sc + w[h:h + 1, :] * jnp.maximum(lg[:, h * tq:(h + 1) * tq], 0.0)
        sc_ref[kc] = sc
        store_planes(kc, sc)
        return carry

    lax.fori_loop(0, kd, score_body, 0)
    kt = ki_ref[pl.ds(pl.multiple_of(kd * ck, ck), ck), :]
    lg = jnp.dot(kt, qis_ref[...], preferred_element_type=F32)
    sc = jnp.zeros((ck, tq), F32)
    for h in range(nh):
        sc = sc + w[h:h + 1, :] * jnp.maximum(lg[:, h * tq:(h + 1) * tq], 0.0)
    kpos = kd * ck + lax.broadcasted_iota(jnp.int32, (ck, tq), 0)
    qpos = q0 + lax.broadcasted_iota(jnp.int32, (ck, tq), 1)
    sc = jnp.where(kpos <= qpos, sc, -jnp.inf)
    sc_ref[kd] = sc
    store_planes(kd, sc)

    live_blocks = nkc * blocks_per_chunk
    for blk in range(n_blocks):
        cand_ref[blk] = jnp.full((F32_SUBLANES, tq), jnp.where(blk < live_blocks, -1, 0), jnp.int32)

    def select_body(i, carry):
        need, ukey_thr = carry
        accs = [jnp.zeros((F32_SUBLANES, tq), jnp.int32) for _ in range(4)]
        for blk in range(n_blocks):
            accs[blk % 4] = accs[blk % 4] + lax.population_count(cand_ref[blk] & planes_ref[i, blk])
        acc = (accs[0] + accs[1]) + (accs[2] + accs[3])
        ones_cnt = jnp.sum(acc, axis=0, keepdims=True)
        take = ones_cnt >= need
        flip = jnp.where(take, 0, -1)
        for blk in range(n_blocks):
            cand_ref[blk] = cand_ref[blk] & (planes_ref[i, blk] ^ flip)
        need = jnp.where(take, need, need - ones_cnt)
        ukey_thr = ukey_thr | jnp.where(take, lax.shift_left(jnp.int32(1), KEY_BITS - 1 - i), 0)
        return need, ukey_thr

    init = (jnp.full((1, tq), top_k, jnp.int32), jnp.zeros((1, tq), jnp.int32))
    _, ukey_thr = lax.fori_loop(0, KEY_BITS, select_body, init)
    thr_bits = jnp.where(ukey_thr < 0, ukey_thr ^ jnp.int32(-2 ** 31), ~ukey_thr)
    thr = lax.bitcast_convert_type(thr_bits, F32)
    qrow = q0 + lax.broadcasted_iota(jnp.int32, (1, tq), 1)
    thr = jnp.where(qrow < top_k, jnp.finfo(F32).min, thr)

    m_ref[...] = jnp.full(m_ref.shape, jnp.finfo(F32).min, F32)
    acc_ref[...] = jnp.zeros(acc_ref.shape, F32)

    def attn_body(kc, carry):
        off = pl.multiple_of(kc * ck, ck)
        s = jnp.dot(ka_ref[pl.ds(off, ck), :], qas_ref[...], preferred_element_type=F32)
        neg = jnp.where(sc_ref[kc] >= thr, 0.0, -jnp.inf)
        for h in range(nh):
            cols = slice(h * tq, (h + 1) * tq)
            sh = s[:, cols] + neg
            cm = jnp.max(sh.reshape(ck // MAX_ROWS, MAX_ROWS, tq), axis=0)
            m_old = m_ref[h:h + 1, :]
            m_new = jnp.maximum(m_old, jnp.max(cm, axis=0, keepdims=True))
            alpha_ref[h:h + 1, :] = jnp.exp2(m_old - m_new)
            m_ref[h:h + 1, :] = m_new
            p_ref[:, cols] = jnp.exp2(sh - m_new).astype(BF16)
        pv = jnp.dot(vext_ref[kc], p_ref[...], preferred_element_type=F32)
        for h in range(nh):
            acc_ref[h] = alpha_ref[h:h + 1, :] * acc_ref[h] + pv[:, h * tq:(h + 1) * tq]
        return carry

    lax.fori_loop(0, nkc, attn_body, 0)

    o_t = []
    for h in range(nh):
        a = acc_ref[h]
        o_t.append(a[0:A_KV_DIM, :] / a[A_KV_DIM:A_KV_DIM + 1, :])
    o = jnp.concatenate(o_t, axis=0).T
    g = ga_ref[...]
    o_ref[...] = (o * (g * _sigmoid(g))).astype(o_ref.dtype)


def _dsa(qit, wit, qat, ga, ki, ka, vat, B, L):
    tq, ck = DSA_Q_COLS, min(DSA_KEY_CHUNK, L)
    top_k = min(TOPK_MAX, L // 4)
    nq = L // tq
    nck = L // ck
    assert L % ck == 0 and ck % BLOCK_KEYS == 0
    col = lambda b, q: (0, b * nq + q)
    kernel = functools.partial(_dsa_kernel, tq=tq, ck=ck, top_k=top_k)
    return pl.pallas_call(
        kernel,
        grid=(B, nq),
        in_specs=[pl.BlockSpec((A_WIDTH, tq), col),
                  pl.BlockSpec((IDX_HEADS, tq), col),
                  pl.BlockSpec((A_WIDTH, tq), col),
                  pl.BlockSpec((tq, A_WIDTH), lambda b, q: (b * nq + q, 0)),
                  pl.BlockSpec((L, IDX_DIM), lambda b, q: (b, 0)),
                  pl.BlockSpec((L, A_KV_DIM), lambda b, q: (b, 0)),
                  pl.BlockSpec((A_KV_DIM, L), lambda b, q: (0, b))],
        out_specs=pl.BlockSpec((tq, A_WIDTH), lambda b, q: (b * nq + q, 0)),
        out_shape=jax.ShapeDtypeStruct((B * L, A_WIDTH), BF16),
        scratch_shapes=[pltpu.VMEM((IDX_DIM, A_HEADS * tq), BF16),
                        pltpu.VMEM((A_HEAD_DIM, A_HEADS * tq), BF16),
                        pltpu.VMEM((nck, DSA_VEXT_ROWS, ck), BF16),
                        pltpu.VMEM((nck, ck, tq), F32),
                        pltpu.VMEM((KEY_BITS, L // BLOCK_KEYS, F32_SUBLANES, tq), jnp.int32),
                        pltpu.VMEM((L // BLOCK_KEYS, F32_SUBLANES, tq), jnp.int32),
                        pltpu.VMEM((A_HEADS, tq), F32),
                        pltpu.VMEM((A_HEADS, tq), F32),
                        pltpu.VMEM((A_HEADS, DSA_VEXT_ROWS, tq), F32),
                        pltpu.VMEM((ck, A_HEADS * tq), BF16)],
        compiler_params=_params("arbitrary", "arbitrary"),
        name="dsa",
    )(qit, wit, qat, ga, ki, ka, vat)


def _gla_kernel(q_ref, k_ref, v_ref, gb_ref, small_ref, wup_ref, bg_ref, gh_ref, o_ref, s_ref,
                *, c, sub):
    @pl.when(pl.program_id(1) == 0)
    def _():
        s_ref[...] = jnp.zeros(s_ref.shape, F32)

    hi = lax.Precision.HIGHEST
    row = lax.broadcasted_iota(jnp.int32, (c, c), 0)
    col = lax.broadcasted_iota(jnp.int32, (c, c), 1)
    tri = jnp.where(col <= row, 1.0, 0.0)
    lane = lax.broadcasted_iota(jnp.int32, (B_KEY_DIM, c), 1)
    rmod = lax.broadcasted_iota(jnp.int32, (c, B_KEY_DIM), 0) % sub
    diff = col - row
    ones = jnp.ones((B_KEY_DIM, c), BF16)
    small = small_ref[...]

    for h in range(B_HEADS):
        kcols = slice(h * B_KEY_DIM, (h + 1) * B_KEY_DIM)
        vcols = slice(h * B_VAL_DIM, (h + 1) * B_VAL_DIM)
        q = q_ref[:, kcols] * (B_KEY_DIM ** -0.5)
        k = k_ref[:, kcols]
        v = v_ref[:, vcols]
        xg = jnp.dot(small, wup_ref[:, kcols], precision=hi, preferred_element_type=F32) + bg_ref[:, kcols]
        la = (jnp.minimum(xg, 0.0) - jnp.log(1.0 + jnp.exp(-jnp.abs(xg)))) * (1.0 / GATE_TAU)
        g = jnp.dot(tri, la, precision=hi, preferred_element_type=F32)
        gt = g.T
        kt = k.T

        o = jnp.dot((q * jnp.exp(g)).astype(BF16), s_ref[h].astype(BF16), preferred_element_type=F32)

        a_rows = [jnp.zeros((sub, c), F32)]
        for i in range(1, c // sub):
            lo = i * sub
            expo = jnp.where(lane < lo, gt[:, lo - 1:lo] - gt, -jnp.inf)
            kti = (kt * jnp.exp(expo)).astype(BF16)
            qi = (q[lo:lo + sub, :] * jnp.exp(g[lo:lo + sub, :] - g[lo - 1:lo, :])).astype(BF16)
            a_rows.append(jnp.dot(qi, kti, preferred_element_type=F32))
        a = jnp.concatenate(a_rows, axis=0)

        for d in range(sub):
            ks = k if d == 0 else pltpu.roll(k, d, axis=0)
            gs = g if d == 0 else pltpu.roll(g, d, axis=0)
            e = jnp.where(rmod >= d, g - gs, -jnp.inf)
            r = jnp.dot((q * ks * jnp.exp(e)).astype(BF16), ones, preferred_element_type=F32)
            a = a + jnp.where(diff == -d, r, 0.0)

        o = o + jnp.dot(a.astype(BF16), v, preferred_element_type=F32)

        glast = gt[:, c - 1:c]
        kend = (kt * jnp.exp(glast - gt)).astype(BF16)
        s_ref[h] = jnp.exp(glast) * s_ref[h] + jnp.dot(kend, v, preferred_element_type=F32)

        gb = gb_ref[:, vcols]
        o_ref[:, vcols] = (_rms(o, gh_ref[...]) * (gb * _sigmoid(gb))).astype(o_ref.dtype)


def _gla(qb, kb, vb, gb, small, wup_pad, b_gate, g_head, B, L):
    c = GLA_CHUNK
    nc = L // c
    row = lambda b, i: (b * nc + i, 0)
    fixed = lambda b, i: (0, 0)
    return pl.pallas_call(
        functools.partial(_gla_kernel, c=c, sub=GLA_SUB),
        grid=(B, nc),
        in_specs=[pl.BlockSpec((c, B_KEY_WIDTH), row),
                  pl.BlockSpec((c, B_KEY_WIDTH), row),
                  pl.BlockSpec((c, B_VAL_WIDTH), row),
                  pl.BlockSpec((c, B_VAL_WIDTH), row),
                  pl.BlockSpec((c, LANES), row),
                  pl.BlockSpec((LANES, B_KEY_WIDTH), fixed),
                  pl.BlockSpec((1, B_KEY_WIDTH), fixed),
                  pl.BlockSpec((1, B_VAL_DIM), fixed)],
        out_specs=pl.BlockSpec((c, B_VAL_WIDTH), row),
        out_shape=jax.ShapeDtypeStruct((B * L, B_VAL_WIDTH), BF16),
        scratch_shapes=[pltpu.VMEM((B_HEADS, B_KEY_DIM, B_VAL_DIM), F32)],
        compiler_params=_params("arbitrary", "arbitrary"),
        name="gla",
    )(qb, kb, vb, gb, small, wup_pad, b_gate.reshape(1, B_KEY_WIDTH), g_head.reshape(1, B_VAL_DIM))


def _merge_kernel(x_ref, p_ref, oa_ref, ob_ref, ma_ref, mb_ref, wpa_ref, wpb_ref, wout_ref,
                  gpost_ref, wple_ref, wpg_ref, gpre_ref, gpost2_ref, o_ref):
    ya = jnp.dot(oa_ref[...], wpa_ref[...], preferred_element_type=F32)
    yb = jnp.dot(ob_ref[...], wpb_ref[...], preferred_element_type=F32)
    y = _sigmoid(ma_ref[...]) * ya + _sigmoid(mb_ref[...]) * yb
    u = jnp.dot(y.astype(BF16), wout_ref[...], preferred_element_type=F32)
    x1 = x_ref[...] + _rms(u, gpost_ref[...])
    gate = jnp.dot(_rms(x1, gpre_ref[...]).astype(BF16), wpg_ref[...], preferred_element_type=F32)
    e = jnp.dot(p_ref[...].astype(BF16), wple_ref[...], preferred_element_type=F32) * _sigmoid(gate)
    o_ref[...] = x1 + _rms(e, gpost2_ref[...])


def _merge(x2d, p2d, oa, ob, ma, mb, wpa, wpb, wout, g_post, wple, wpg, g_pre2, g_post2):
    T = x2d.shape[0]
    tm = MERGE_ROWS
    rows = lambda width: pl.BlockSpec((tm, width), lambda i: (i, 0))
    full = lambda a: pl.BlockSpec(a.shape, lambda i: (0, 0))
    vec = lambda a: a.reshape(1, D_MODEL)
    args = (x2d, p2d, oa, ob, ma, mb, wpa, wpb, wout, vec(g_post), wple, wpg, vec(g_pre2), vec(g_post2))
    in_specs = [rows(D_MODEL), rows(PLE_DIM), rows(A_WIDTH), rows(B_VAL_WIDTH), rows(D_MODEL), rows(D_MODEL)]
    in_specs += [full(a) for a in args[6:]]
    return pl.pallas_call(
        _merge_kernel,
        grid=(T // tm,),
        in_specs=in_specs,
        out_specs=rows(D_MODEL),
        out_shape=jax.ShapeDtypeStruct((T, D_MODEL), F32),
        compiler_params=_params("arbitrary"),
        name="merge",
    )(*args)


def _split_w_in(w_in):
    offs = [0]
    for s in IN_SPLITS:
        offs.append(offs[-1] + s)
    col = lambda i: w_in[:, offs[i]:offs[i + 1]]
    qa, ka, va, qi, ki, wi, ga, qb, kb, vb, gdown, gb, ma, mb = [col(i) for i in range(len(IN_SPLITS))]
    pad = jnp.zeros((D_MODEL, LANES - GATE_RANK), w_in.dtype)
    small = jnp.concatenate([gdown, pad], axis=1)
    groups = [qa.T, ka, va.T, qi.T, ki, wi.T, small, ga, qb, kb, vb, gb, ma, mb]
    dtypes = [BF16, BF16, BF16, BF16, BF16, F32, F32, F32, F32, F32, BF16, F32, F32, F32]
    transposed = [True, False, True, True, False, True] + [False] * 8
    return [g.astype(BF16) for g in groups], dtypes, transposed


def _layer(x2d, p2d, B, L, g_pre, w_in, w_gate_up, b_gate, g_gla_head, w_proj_a, w_proj_b,
           w_out, g_post, w_ple, w_ple_gate, g_ple_pre, g_ple_post):
    groups, dtypes, transposed = _split_w_in(w_in)
    qat, ka, vat, qit, ki, wit, small, ga, qb, kb, vb, gb, ma, mb = _in_proj(
        x2d, g_pre, groups, dtypes, transposed)
    oa = _dsa(qit, wit, qat, ga, ki, ka, vat, B, L)
    wup_pad = jnp.zeros((LANES, B_KEY_WIDTH), F32).at[SMALL_GD_OFF:SMALL_GD_OFF + GATE_RANK].set(w_gate_up)
    ob = _gla(qb, kb, vb, gb, small, wup_pad, b_gate, g_gla_head, B, L)
    bf = lambda a: a.astype(BF16)
    return _merge(x2d, p2d, oa, ob, ma, mb, bf(w_proj_a), bf(w_proj_b), bf(w_out), g_post,
                  bf(w_ple), bf(w_ple_gate), g_ple_pre, g_ple_post)


def kernel(x, p, g_pre, w_in, w_gate_up, b_gate, g_gla_head, w_proj_a, w_proj_b, w_out, g_post,
           w_ple, w_ple_gate, g_ple_pre, g_ple_post):
    B, L, _ = x.shape
    depth = p.shape[0]
    x2d = x.reshape(B * L, D_MODEL)
    for i in range(depth):
        x2d = _layer(x2d, p[i].reshape(B * L, PLE_DIM), B, L, g_pre[i], w_in[i], w_gate_up[i],
                     b_gate[i], g_gla_head[i], w_proj_a[i], w_proj_b[i], w_out[i], g_post[i],
                     w_ple[i], w_ple_gate[i], g_ple_pre[i], g_ple_post[i])
    return x2d.reshape(B, L, D_MODEL)
```

```python
import functools

import jax
import jax.numpy as jnp
from jax import lax
from jax.experimental import pallas as pl
from jax.experimental.pallas import tpu as pltpu

F32 = jnp.float32
BF16 = jnp.bfloat16

D_MODEL = 1024
PLE_DIM = 256
A_HEADS = 8
A_HEAD_DIM = 64
A_WIDTH = A_HEADS * A_HEAD_DIM
A_KV_DIM = 64
IDX_HEADS = 8
IDX_DIM = 64
TOPK_MAX = 256
B_HEADS = 4
B_KEY_DIM = 128
B_VAL_DIM = 256
B_KEY_WIDTH = B_HEADS * B_KEY_DIM
B_VAL_WIDTH = B_HEADS * B_VAL_DIM
GATE_RANK = 16
GATE_TAU = 16.0
EPS = 1e-6

IN_SPLITS = (A_WIDTH, A_KV_DIM, A_KV_DIM, IDX_HEADS * IDX_DIM, IDX_DIM, IDX_HEADS, A_WIDTH,
             B_KEY_WIDTH, B_KEY_WIDTH, B_VAL_WIDTH, GATE_RANK, B_VAL_WIDTH,
             D_MODEL, D_MODEL)

VMEM_LIMIT_BYTES = 56 * 1024 * 1024
LANES = 128
BF16_SUBLANES = 16

SMALL_GD_OFF = 0

IN_PROJ_ROWS = 256
MERGE_ROWS = 256
DSA_Q_COLS = 256
DSA_KEY_CHUNK = 512
DSA_VEXT_ROWS = A_KV_DIM + BF16_SUBLANES
GLA_CHUNK = 128
GLA_SUB = 16

NT_DIMS = (((1,), (1,)), ((), ()))


def _sigmoid(x):
    return 1.0 / (1.0 + jnp.exp(-x))


def _rms(x, g):
    ms = jnp.mean(x * x, axis=-1, keepdims=True)
    return x * lax.rsqrt(ms + EPS) * g


def _params(*sem):
    return pltpu.CompilerParams(dimension_semantics=sem, vmem_limit_bytes=VMEM_LIMIT_BYTES)


def _in_proj_kernel(x_ref, g_ref, *refs, transposed):
    n_out = len(transposed)
    w_refs, o_refs = refs[:n_out], refs[n_out:]
    h = _rms(x_ref[...], g_ref[...]).astype(BF16)
    for w_ref, o_ref, tr in zip(w_refs, o_refs, transposed):
        if tr:
            z = lax.dot_general(w_ref[...], h, NT_DIMS, preferred_element_type=F32)
        else:
            z = jnp.dot(h, w_ref[...], preferred_element_type=F32)
        o_ref[...] = z.astype(o_ref.dtype)


def _in_proj(x2d, g_pre, w_groups, out_dtypes, transposed):
    T = x2d.shape[0]
    tm = IN_PROJ_ROWS
    in_specs = [pl.BlockSpec((tm, D_MODEL), lambda i: (i, 0)),
                pl.BlockSpec((1, D_MODEL), lambda i: (0, 0))]
    in_specs += [pl.BlockSpec(w.shape, lambda i: (0, 0)) for w in w_groups]
    out_specs, out_shape = [], []
    for w, dt, tr in zip(w_groups, out_dtypes, transposed):
        if tr:
            out_specs.append(pl.BlockSpec((w.shape[0], tm), lambda i: (0, i)))
            out_shape.append(jax.ShapeDtypeStruct((w.shape[0], T), dt))
        else:
            out_specs.append(pl.BlockSpec((tm, w.shape[1]), lambda i: (i, 0)))
            out_shape.append(jax.ShapeDtypeStruct((T, w.shape[1]), dt))
    return pl.pallas_call(
        functools.partial(_in_proj_kernel, transposed=tuple(transposed)),
        grid=(T // tm,),
        in_specs=in_specs,
        out_specs=out_specs,
        out_shape=out_shape,
        compiler_params=_params("arbitrary"),
        name="in_proj",
    )(x2d, g_pre.reshape(1, D_MODEL), *w_groups)


LOG2E = 1.4426950408889634
MAX_ROWS = 4 * 8
F32_SUBLANES = 8
KEY_BITS = 32
BLOCK_KEYS = KEY_BITS * F32_SUBLANES


def _bit_transpose(words):
    a = list(words)
    j, m = 16, 0x0000FFFF
    while j:
        mask = jnp.int32(m if m < 2 ** 31 else m - 2 ** 32)
        k = 0
        while k < KEY_BITS:
            t = (a[k] ^ lax.shift_right_logical(a[k + j], jnp.int32(j))) & mask
            a[k] = a[k] ^ t
            a[k + j] = a[k + j] ^ lax.shift_left(t, jnp.int32(j))
            k = (k + j + 1) & ~j
        j >>= 1
        m = (m ^ (m << j)) & 0xFFFFFFFF
    return a


def _dsa_kernel(qit_ref, wit_ref, qat_ref, ga_ref, ki_ref, ka_ref, vat_ref, o_ref,
                qis_ref, qas_ref, kaug_ref, knorm_ref, vext_ref, sc_ref, planes_ref, cand_ref,
                m_ref, alpha_ref, acc_ref, p_ref, *, tq, ck, top_k):
    qb = pl.program_id(1)
    nh = A_HEADS
    idx_scale = (IDX_HEADS ** -0.5) * (IDX_DIM ** -0.5)
    attn_scale = (A_HEAD_DIM ** -0.5) * LOG2E
    n_chunks = vext_ref.shape[0]
    n_blocks = cand_ref.shape[0]
    blocks_per_chunk = ck // BLOCK_KEYS

    @pl.when(qb == 0)
    def _():
        for j in range(n_chunks):
            vext_ref[j, 0:A_KV_DIM, :] = vat_ref[:, j * ck:(j + 1) * ck]
            vext_ref[j, A_KV_DIM:, :] = jnp.ones((DSA_VEXT_ROWS - A_KV_DIM, ck), BF16)
        planes_ref[...] = jnp.zeros(planes_ref.shape, jnp.int32)
        kaug_ref[:, 0:A_KV_DIM] = ka_ref[...]
        one_col = lax.broadcasted_iota(jnp.int32, (kaug_ref.shape[0], LANES - A_KV_DIM), 1) == 0
        kaug_ref[:, A_KV_DIM:] = jnp.where(one_col, 1.0, 0.0).astype(BF16)
        qas_ref[A_HEAD_DIM + BF16_SUBLANES:, :] = jnp.zeros((LANES - A_HEAD_DIM - BF16_SUBLANES, nh * tq), BF16)
        k32 = ka_ref[...].astype(F32)
        knorm_ref[...] = jnp.sqrt(jnp.max(jnp.sum(k32 * k32, axis=1, keepdims=True), axis=0, keepdims=True))

    qa = (qat_ref[...].astype(F32) * attn_scale).astype(BF16)
    q32 = qa.astype(F32).reshape(nh, A_HEAD_DIM, tq)
    bound = jnp.sqrt(jnp.sum(q32 * q32, axis=1)) * knorm_ref[...]
    first = lax.broadcasted_iota(jnp.int32, (BF16_SUBLANES, tq), 0) == 0
    for h in range(nh):
        cols = slice(h * tq, (h + 1) * tq)
        qis_ref[:, cols] = qit_ref[h * IDX_DIM:(h + 1) * IDX_DIM, :]
        qas_ref[0:A_HEAD_DIM, cols] = qa[h * A_HEAD_DIM:(h + 1) * A_HEAD_DIM, :]
        qas_ref[A_HEAD_DIM:A_HEAD_DIM + BF16_SUBLANES, cols] = jnp.where(first, -bound[h:h + 1, :], 0.0).astype(BF16)

    q0 = qb * tq
    kd = q0 // ck
    nkc = kd + 1
    w = wit_ref[...] * idx_scale

    def store_planes(kc, sc):
        bits = lax.bitcast_convert_type(sc, jnp.int32)
        ukey = bits ^ ((bits >> 31) | jnp.int32(-2 ** 31))
        for j in range(blocks_per_chunk):
            rows = [ukey[(j * KEY_BITS + e) * F32_SUBLANES:(j * KEY_BITS + e + 1) * F32_SUBLANES, :]
                    for e in range(KEY_BITS)]
            planes = _bit_transpose(rows)
            for b in range(KEY_BITS):
                planes_ref[b, kc * blocks_per_chunk + j] = planes[b]

    def score_body(kc, carry):
        kt = ki_ref[pl.ds(pl.multiple_of(kc * ck, ck), ck), :]
        lg = jnp.dot(kt, qis_ref[...], preferred_element_type=F32)
        sc = jnp.zeros((ck, tq), F32)
        for h in range(nh):
            sc = sc + w[h:h + 1, :] * jnp.maximum(lg[:, h * tq:(h + 1) * tq], 0.0)
        sc_ref[kc] = sc
        store_planes(kc, sc)
        return carry

    lax.fori_loop(0, kd, score_body, 0)
    kt = ki_ref[pl.ds(pl.multiple_of(kd * ck, ck), ck), :]
    lg = jnp.dot(kt, qis_ref[...], preferred_element_type=F32)
    sc = jnp.zeros((ck, tq), F32)
    for h in range(nh):
        sc = sc + w[h:h + 1, :] * jnp.maximum(lg[:, h * tq:(h + 1) * tq], 0.0)
    kpos = kd * ck + lax.broadcasted_iota(jnp.int32, (ck, tq), 0)
    qpos = q0 + lax.broadcasted_iota(jnp.int32, (ck, tq), 1)
    sc = jnp.where(kpos <= qpos, sc, -jnp.inf)
    sc_ref[kd] = sc
    store_planes(kd, sc)

    live_blocks = nkc * blocks_per_chunk
    for blk in range(n_blocks):
        cand_ref[blk] = jnp.full((F32_SUBLANES, tq), jnp.where(blk < live_blocks, -1, 0), jnp.int32)

    def select_body(i, carry):
        need, ukey_thr = carry
        accs = [jnp.zeros((F32_SUBLANES, tq), jnp.int32) for _ in range(4)]
        for blk in range(n_blocks):
            accs[blk % 4] = accs[blk % 4] + lax.population_count(cand_ref[blk] & planes_ref[i, blk])
        acc = (accs[0] + accs[1]) + (accs[2] + accs[3])
        ones_cnt = jnp.sum(acc, axis=0, keepdims=True)
        take = ones_cnt >= need
        flip = jnp.where(take, 0, -1)
        for blk in range(n_blocks):
            cand_ref[blk] = cand_ref[blk] & (planes_ref[i, blk] ^ flip)
        need = jnp.where(take, need, need - ones_cnt)
        ukey_thr = ukey_thr | jnp.where(take, lax.shift_left(jnp.int32(1), KEY_BITS - 1 - i), 0)
        return need, ukey_thr

    init = (jnp.full((1, tq), top_k, jnp.int32), jnp.zeros((1, tq), jnp.int32))
    _, ukey_thr = lax.fori_loop(0, KEY_BITS, select_body, init)
    thr_bits = jnp.where(ukey_thr < 0, ukey_thr ^ jnp.int32(-2 ** 31), ~ukey_thr)
    thr = lax.bitcast_convert_type(thr_bits, F32)
    qrow = q0 + lax.broadcasted_iota(jnp.int32, (1, tq), 1)
    thr = jnp.where(qrow < top_k, jnp.finfo(F32).min, thr)

    acc_ref[...] = jnp.zeros(acc_ref.shape, F32)

    def fast_body(kc, carry):
        off = pl.multiple_of(kc * ck, ck)
        s = jnp.dot(kaug_ref[pl.ds(off, ck), :], qas_ref[...], preferred_element_type=F32)
        neg = jnp.where(sc_ref[kc] >= thr, 0.0, -jnp.inf)
        for h in range(nh):
            cols = slice(h * tq, (h + 1) * tq)
            p_ref[:, cols] = jnp.exp2(s[:, cols] + neg).astype(BF16)
        pv = jnp.dot(vext_ref[kc], p_ref[...], preferred_element_type=F32)
        for h in range(nh):
            acc_ref[h] = acc_ref[h] + pv[:, h * tq:(h + 1) * tq]
        return carry

    lax.fori_loop(0, nkc, fast_body, 0)

    denom = acc_ref[0, A_KV_DIM:A_KV_DIM + F32_SUBLANES, :]
    for h in range(1, nh):
        denom = jnp.minimum(denom, acc_ref[h, A_KV_DIM:A_KV_DIM + F32_SUBLANES, :])
    healthy = jnp.min(denom) >= 2.0 ** -100

    @pl.when(jnp.logical_not(healthy))
    def _():
        m_ref[...] = jnp.full(m_ref.shape, jnp.finfo(F32).min, F32)
        acc_ref[...] = jnp.zeros(acc_ref.shape, F32)

        def attn_body(kc, carry):
            off = pl.multiple_of(kc * ck, ck)
            s = jnp.dot(ka_ref[pl.ds(off, ck), :], qas_ref[0:A_HEAD_DIM, :], preferred_element_type=F32)
            neg = jnp.where(sc_ref[kc] >= thr, 0.0, -jnp.inf)
            for h in range(nh):
                cols = slice(h * tq, (h + 1) * tq)
                sh = s[:, cols] + neg
                cm = jnp.max(sh.reshape(ck // MAX_ROWS, MAX_ROWS, tq), axis=0)
                m_old = m_ref[h:h + 1, :]
                m_new = jnp.maximum(m_old, jnp.max(cm, axis=0, keepdims=True))
                alpha_ref[h:h + 1, :] = jnp.exp2(m_old - m_new)
                m_ref[h:h + 1, :] = m_new
                p_ref[:, cols] = jnp.exp2(sh - m_new).astype(BF16)
            pv = jnp.dot(vext_ref[kc], p_ref[...], preferred_element_type=F32)
            for h in range(nh):
                acc_ref[h] = alpha_ref[h:h + 1, :] * acc_ref[h] + pv[:, h * tq:(h + 1) * tq]
            return carry

        lax.fori_loop(0, nkc, attn_body, 0)

    o_t = []
    for h in range(nh):
        a = acc_ref[h]
        o_t.append(a[0:A_KV_DIM, :] / a[A_KV_DIM:A_KV_DIM + 1, :])
    o = jnp.concatenate(o_t, axis=0).T
    g = ga_ref[...]
    o_ref[...] = (o * (g * _sigmoid(g))).astype(o_ref.dtype)


def _dsa(qit, wit, qat, ga, ki, ka, vat, B, L):
    tq, ck = DSA_Q_COLS, min(DSA_KEY_CHUNK, L)
    top_k = min(TOPK_MAX, L // 4)
    nq = L // tq
    nck = L // ck
    assert L % ck == 0 and ck % BLOCK_KEYS == 0
    col = lambda b, q: (0, b * nq + q)
    kernel = functools.partial(_dsa_kernel, tq=tq, ck=ck, top_k=top_k)
    return pl.pallas_call(
        kernel,
        grid=(B, nq),
        in_specs=[pl.BlockSpec((A_WIDTH, tq), col),
                  pl.BlockSpec((IDX_HEADS, tq), col),
                  pl.BlockSpec((A_WIDTH, tq), col),
                  pl.BlockSpec((tq, A_WIDTH), lambda b, q: (b * nq + q, 0)),
                  pl.BlockSpec((L, IDX_DIM), lambda b, q: (b, 0)),
                  pl.BlockSpec((L, A_KV_DIM), lambda b, q: (b, 0)),
                  pl.BlockSpec((A_KV_DIM, L), lambda b, q: (0, b))],
        out_specs=pl.BlockSpec((tq, A_WIDTH), lambda b, q: (b * nq + q, 0)),
        out_shape=jax.ShapeDtypeStruct((B * L, A_WIDTH), BF16),
        scratch_shapes=[pltpu.VMEM((IDX_DIM, A_HEADS * tq), BF16),
                        pltpu.VMEM((LANES, A_HEADS * tq), BF16),
                        pltpu.VMEM((L, LANES), BF16),
                        pltpu.VMEM((1, 1), F32),
                        pltpu.VMEM((nck, DSA_VEXT_ROWS, ck), BF16),
                        pltpu.VMEM((nck, ck, tq), F32),
                        pltpu.VMEM((KEY_BITS, L // BLOCK_KEYS, F32_SUBLANES, tq), jnp.int32),
                        pltpu.VMEM((L // BLOCK_KEYS, F32_SUBLANES, tq), jnp.int32),
                        pltpu.VMEM((A_HEADS, tq), F32),
                        pltpu.VMEM((A_HEADS, tq), F32),
                        pltpu.VMEM((A_HEADS, DSA_VEXT_ROWS, tq), F32),
                        pltpu.VMEM((ck, A_HEADS * tq), BF16)],
        compiler_params=_params("arbitrary", "arbitrary"),
        name="dsa",
    )(qit, wit, qat, ga, ki, ka, vat)


def _gla_kernel(q_ref, k_ref, v_ref, gb_ref, small_ref, wup_ref, bg_ref, gh_ref, o_ref, s_ref,
                *, c, sub):
    @pl.when(pl.program_id(1) == 0)
    def _():
        s_ref[...] = jnp.zeros(s_ref.shape, F32)

    hi = lax.Precision.HIGHEST
    row = lax.broadcasted_iota(jnp.int32, (c, c), 0)
    col = lax.broadcasted_iota(jnp.int32, (c, c), 1)
    tri = jnp.where(col <= row, 1.0, 0.0)
    lane = lax.broadcasted_iota(jnp.int32, (B_KEY_DIM, c), 1)
    rmod = lax.broadcasted_iota(jnp.int32, (c, B_KEY_DIM), 0) % sub
    diff = col - row
    ones = jnp.ones((B_KEY_DIM, c), BF16)
    small = small_ref[...]

    for h in range(B_HEADS):
        kcols = slice(h * B_KEY_DIM, (h + 1) * B_KEY_DIM)
        vcols = slice(h * B_VAL_DIM, (h + 1) * B_VAL_DIM)
        q = q_ref[:, kcols] * (B_KEY_DIM ** -0.5)
        k = k_ref[:, kcols]
        v = v_ref[:, vcols]
        xg = jnp.dot(small, wup_ref[:, kcols], precision=hi, preferred_element_type=F32) + bg_ref[:, kcols]
        la = (jnp.minimum(xg, 0.0) - jnp.log(1.0 + jnp.exp(-jnp.abs(xg)))) * (1.0 / GATE_TAU)
        g = jnp.dot(tri, la, precision=hi, preferred_element_type=F32)
        gt = g.T
        kt = k.T

        o = jnp.dot((q * jnp.exp(g)).astype(BF16), s_ref[h].astype(BF16), preferred_element_type=F32)

        a_rows = [jnp.zeros((sub, c), F32)]
        for i in range(1, c // sub):
            lo = i * sub
            expo = jnp.where(lane < lo, gt[:, lo - 1:lo] - gt, -jnp.inf)
            kti = (kt * jnp.exp(expo)).astype(BF16)
            qi = (q[lo:lo + sub, :] * jnp.exp(g[lo:lo + sub, :] - g[lo - 1:lo, :])).astype(BF16)
            a_rows.append(jnp.dot(qi, kti, preferred_element_type=F32))
        a = jnp.concatenate(a_rows, axis=0)

        for d in range(sub):
            ks = k if d == 0 else pltpu.roll(k, d, axis=0)
            gs = g if d == 0 else pltpu.roll(g, d, axis=0)
            e = jnp.where(rmod >= d, g - gs, -jnp.inf)
            r = jnp.dot((q * ks * jnp.exp(e)).astype(BF16), ones, preferred_element_type=F32)
            a = a + jnp.where(diff == -d, r, 0.0)

        o = o + jnp.dot(a.astype(BF16), v, preferred_element_type=F32)

        glast = gt[:, c - 1:c]
        kend = (kt * jnp.exp(glast - gt)).astype(BF16)
        s_ref[h] = jnp.exp(glast) * s_ref[h] + jnp.dot(kend, v, preferred_element_type=F32)

        gb = gb_ref[:, vcols]
        o_ref[:, vcols] = (_rms(o, gh_ref[...]) * (gb * _sigmoid(gb))).astype(o_ref.dtype)


def _gla(qb, kb, vb, gb, small, wup_pad, b_gate, g_head, B, L):
    c = GLA_CHUNK
    nc = L // c
    row = lambda b, i: (b * nc + i, 0)
    fixed = lambda b, i: (0, 0)
    return pl.pallas_call(
        functools.partial(_gla_kernel, c=c, sub=GLA_SUB),
        grid=(B, nc),
        in_specs=[pl.BlockSpec((c, B_KEY_WIDTH), row),
                  pl.BlockSpec((c, B_KEY_WIDTH), row),
                  pl.BlockSpec((c, B_VAL_WIDTH), row),
                  pl.BlockSpec((c, B_VAL_WIDTH), row),
                  pl.BlockSpec((c, LANES), row),
                  pl.BlockSpec((LANES, B_KEY_WIDTH), fixed),
                  pl.BlockSpec((1, B_KEY_WIDTH), fixed),
                  pl.BlockSpec((1, B_VAL_DIM), fixed)],
        out_specs=pl.BlockSpec((c, B_VAL_WIDTH), row),
        out_shape=jax.ShapeDtypeStruct((B * L, B_VAL_WIDTH), BF16),
        scratch_shapes=[pltpu.VMEM((B_HEADS, B_KEY_DIM, B_VAL_DIM), F32)],
        compiler_params=_params("arbitrary", "arbitrary"),
        name="gla",
    )(qb, kb, vb, gb, small, wup_pad, b_gate.reshape(1, B_KEY_WIDTH), g_head.reshape(1, B_VAL_DIM))


def _merge_kernel(x_ref, p_ref, oa_ref, ob_ref, ma_ref, mb_ref, wpa_ref, wpb_ref, wout_ref,
                  gpost_ref, wple_ref, wpg_ref, gpre_ref, gpost2_ref, o_ref):
    ya = jnp.dot(oa_ref[...], wpa_ref[...], preferred_element_type=F32)
    yb = jnp.dot(ob_ref[...], wpb_ref[...], preferred_element_type=F32)
    y = _sigmoid(ma_ref[...]) * ya + _sigmoid(mb_ref[...]) * yb
    u = jnp.dot(y.astype(BF16), wout_ref[...], preferred_element_type=F32)
    x1 = x_ref[...] + _rms(u, gpost_ref[...])
    gate = jnp.dot(_rms(x1, gpre_ref[...]).astype(BF16), wpg_ref[...], preferred_element_type=F32)
    e = jnp.dot(p_ref[...].astype(BF16), wple_ref[...], preferred_element_type=F32) * _sigmoid(gate)
    o_ref[...] = x1 + _rms(e, gpost2_ref[...])


def _merge(x2d, p2d, oa, ob, ma, mb, wpa, wpb, wout, g_post, wple, wpg, g_pre2, g_post2):
    T = x2d.shape[0]
    tm = MERGE_ROWS
    rows = lambda width: pl.BlockSpec((tm, width), lambda i: (i, 0))
    full = lambda a: pl.BlockSpec(a.shape, lambda i: (0, 0))
    vec = lambda a: a.reshape(1, D_MODEL)
    args = (x2d, p2d, oa, ob, ma, mb, wpa, wpb, wout, vec(g_post), wple, wpg, vec(g_pre2), vec(g_post2))
    in_specs = [rows(D_MODEL), rows(PLE_DIM), rows(A_WIDTH), rows(B_VAL_WIDTH), rows(D_MODEL), rows(D_MODEL)]
    in_specs += [full(a) for a in args[6:]]
    return pl.pallas_call(
        _merge_kernel,
        grid=(T // tm,),
        in_specs=in_specs,
        out_specs=rows(D_MODEL),
        out_shape=jax.ShapeDtypeStruct((T, D_MODEL), F32),
        compiler_params=_params("arbitrary"),
        name="merge",
    )(*args)


def _split_w_in(w_in):
    offs = [0]
    for s in IN_SPLITS:
        offs.append(offs[-1] + s)
    col = lambda i: w_in[:, offs[i]:offs[i + 1]]
    qa, ka, va, qi, ki, wi, ga, qb, kb, vb, gdown, gb, ma, mb = [col(i) for i in range(len(IN_SPLITS))]
    pad = jnp.zeros((D_MODEL, LANES - GATE_RANK), w_in.dtype)
    small = jnp.concatenate([gdown, pad], axis=1)
    groups = [qa.T, ka, va.T, qi.T, ki, wi.T, small, ga, qb, kb, vb, gb, ma, mb]
    dtypes = [BF16, BF16, BF16, BF16, BF16, F32, F32, F32, F32, F32, BF16, F32, F32, F32]
    transposed = [True, False, True, True, False, True] + [False] * 8
    return [g.astype(BF16) for g in groups], dtypes, transposed


def _layer(x2d, p2d, B, L, g_pre, w_in, w_gate_up, b_gate, g_gla_head, w_proj_a, w_proj_b,
           w_out, g_post, w_ple, w_ple_gate, g_ple_pre, g_ple_post):
    groups, dtypes, transposed = _split_w_in(w_in)
    qat, ka, vat, qit, ki, wit, small, ga, qb, kb, vb, gb, ma, mb = _in_proj(
        x2d, g_pre, groups, dtypes, transposed)
    oa = _dsa(qit, wit, qat, ga, ki, ka, vat, B, L)
    wup_pad = jnp.zeros((LANES, B_KEY_WIDTH), F32).at[SMALL_GD_OFF:SMALL_GD_OFF + GATE_RANK].set(w_gate_up)
    ob = _gla(qb, kb, vb, gb, small, wup_pad, b_gate, g_gla_head, B, L)
    bf = lambda a: a.astype(BF16)
    return _merge(x2d, p2d, oa, ob, ma, mb, bf(w_proj_a), bf(w_proj_b), bf(w_out), g_post,
                  bf(w_ple), bf(w_ple_gate), g_ple_pre, g_ple_post)


def kernel(x, p, g_pre, w_in, w_gate_up, b_gate, g_gla_head, w_proj_a, w_proj_b, w_out, g_post,
           w_ple, w_ple_gate, g_ple_pre, g_ple_post):
    B, L, _ = x.shape
    depth = p.shape[0]
    x2d = x.reshape(B * L, D_MODEL)
    for i in range(depth):
        x2d = _layer(x2d, p[i].reshape(B * L, PLE_DIM), B, L, g_pre[i], w_in[i], w_gate_up[i],
                     b_gate[i], g_gla_head[i], w_proj_a[i], w_proj_b[i], w_out[i], g_post[i],
                     w_ple[i], w_ple_gate[i], g_ple_pre[i], g_ple_post[i])
    return x2d.reshape(B, L, D_MODEL)
```

```python
import functools

import jax
import jax.numpy as jnp
from jax import lax
from jax.experimental import pallas as pl
from jax.experimental.pallas import tpu as pltpu

F32 = jnp.float32
BF16 = jnp.bfloat16

D_MODEL = 1024
PLE_DIM = 256
A_HEADS = 8
A_HEAD_DIM = 64
A_WIDTH = A_HEADS * A_HEAD_DIM
A_KV_DIM = 64
IDX_HEADS = 8
IDX_DIM = 64
TOPK_MAX = 256
B_HEADS = 4
B_KEY_DIM = 128
B_VAL_DIM = 256
B_KEY_WIDTH = B_HEADS * B_KEY_DIM
B_VAL_WIDTH = B_HEADS * B_VAL_DIM
GATE_RANK = 16
GATE_TAU = 16.0
EPS = 1e-6

IN_SPLITS = (A_WIDTH, A_KV_DIM, A_KV_DIM, IDX_HEADS * IDX_DIM, IDX_DIM, IDX_HEADS, A_WIDTH,
             B_KEY_WIDTH, B_KEY_WIDTH, B_VAL_WIDTH, GATE_RANK, B_VAL_WIDTH,
             D_MODEL, D_MODEL)

VMEM_LIMIT_BYTES = 56 * 1024 * 1024
LANES = 128
BF16_SUBLANES = 16

SMALL_GD_OFF = 0

IN_PROJ_ROWS = 256
MERGE_ROWS = 512
MERGE_PARTS = 2
DSA_Q_COLS = 256
DSA_KEY_CHUNK = 512
DSA_VEXT_ROWS = A_KV_DIM + BF16_SUBLANES
GLA_CHUNK = 128
GLA_SUB = 16
GLA_BAND = 4

NT_DIMS = (((1,), (1,)), ((), ()))


def _sigmoid(x):
    return 1.0 / (1.0 + jnp.exp(-x))


def _rms(x, g):
    ms = jnp.mean(x * x, axis=-1, keepdims=True)
    return x * lax.rsqrt(ms + EPS) * g


def _params(*sem):
    return pltpu.CompilerParams(dimension_semantics=sem, vmem_limit_bytes=VMEM_LIMIT_BYTES)


def _in_proj_kernel(x_ref, g_ref, *refs, transposed):
    n_out = len(transposed)
    w_refs, o_refs = refs[:n_out], refs[n_out:]
    h = _rms(x_ref[...], g_ref[...]).astype(BF16)
    for w_ref, o_ref, tr in zip(w_refs, o_refs, transposed):
        if tr:
            z = lax.dot_general(w_ref[...], h, NT_DIMS, preferred_element_type=F32)
        else:
            z = jnp.dot(h, w_ref[...], preferred_element_type=F32)
        o_ref[...] = z.astype(o_ref.dtype)


def _in_proj(x2d, g_pre, w_groups, out_dtypes, transposed):
    T = x2d.shape[0]
    tm = IN_PROJ_ROWS
    in_specs = [pl.BlockSpec((tm, D_MODEL), lambda i: (i, 0)),
                pl.BlockSpec((1, D_MODEL), lambda i: (0, 0))]
    in_specs += [pl.BlockSpec(w.shape, lambda i: (0, 0)) for w in w_groups]
    out_specs, out_shape = [], []
    for w, dt, tr in zip(w_groups, out_dtypes, transposed):
        if tr:
            out_specs.append(pl.BlockSpec((w.shape[0], tm), lambda i: (0, i)))
            out_shape.append(jax.ShapeDtypeStruct((w.shape[0], T), dt))
        else:
            out_specs.append(pl.BlockSpec((tm, w.shape[1]), lambda i: (i, 0)))
            out_shape.append(jax.ShapeDtypeStruct((T, w.shape[1]), dt))
    return pl.pallas_call(
        functools.partial(_in_proj_kernel, transposed=tuple(transposed)),
        grid=(T // tm,),
        in_specs=in_specs,
        out_specs=out_specs,
        out_shape=out_shape,
        compiler_params=_params("arbitrary"),
        name="in_proj",
    )(x2d, g_pre.reshape(1, D_MODEL), *w_groups)


LOG2E = 1.4426950408889634
MAX_ROWS = 4 * 8
F32_SUBLANES = 8
KEY_BITS = 32
BLOCK_KEYS = KEY_BITS * F32_SUBLANES


def _bit_transpose(words):
    a = list(words)
    j, m = 16, 0x0000FFFF
    while j:
        mask = jnp.int32(m if m < 2 ** 31 else m - 2 ** 32)
        k = 0
        while k < KEY_BITS:
            t = (a[k] ^ lax.shift_right_logical(a[k + j], jnp.int32(j))) & mask
            a[k] = a[k] ^ t
            a[k + j] = a[k + j] ^ lax.shift_left(t, jnp.int32(j))
            k = (k + j + 1) & ~j
        j >>= 1
        m = (m ^ (m << j)) & 0xFFFFFFFF
    return a


def _dsa_kernel(qit_ref, wit_ref, qat_ref, ga_ref, ki_ref, ka_ref, vat_ref, o_ref,
                qis_ref, qas_ref, kaug_ref, knorm_ref, vext_ref, sc_ref, planes_ref, cand_ref,
                m_ref, alpha_ref, acc_ref, p_ref, *, tq, ck, top_k):
    qb = pl.program_id(1)
    nh = A_HEADS
    idx_scale = (IDX_HEADS ** -0.5) * (IDX_DIM ** -0.5)
    attn_scale = (A_HEAD_DIM ** -0.5) * LOG2E
    n_chunks = vext_ref.shape[0]
    n_blocks = cand_ref.shape[0]
    blocks_per_chunk = ck // BLOCK_KEYS

    @pl.when(qb == 0)
    def _():
        for j in range(n_chunks):
            vext_ref[j, 0:A_KV_DIM, :] = vat_ref[:, j * ck:(j + 1) * ck]
            vext_ref[j, A_KV_DIM:, :] = jnp.ones((DSA_VEXT_ROWS - A_KV_DIM, ck), BF16)
        planes_ref[...] = jnp.zeros(planes_ref.shape, jnp.int32)
        kaug_ref[:, 0:A_KV_DIM] = ka_ref[...]
        one_col = lax.broadcasted_iota(jnp.int32, (kaug_ref.shape[0], LANES - A_KV_DIM), 1) == 0
        kaug_ref[:, A_KV_DIM:] = jnp.where(one_col, 1.0, 0.0).astype(BF16)
        qas_ref[A_HEAD_DIM + BF16_SUBLANES:, :] = jnp.zeros((LANES - A_HEAD_DIM - BF16_SUBLANES, nh * tq), BF16)
        k32 = ka_ref[...].astype(F32)
        knorm_ref[...] = jnp.sqrt(jnp.max(jnp.sum(k32 * k32, axis=1, keepdims=True), axis=0, keepdims=True))

    qa = (qat_ref[...].astype(F32) * attn_scale).astype(BF16)
    q32 = qa.astype(F32).reshape(nh, A_HEAD_DIM, tq)
    bound = jnp.sqrt(jnp.sum(q32 * q32, axis=1)) * knorm_ref[...]
    first = lax.broadcasted_iota(jnp.int32, (BF16_SUBLANES, tq), 0) == 0
    for h in range(nh):
        cols = slice(h * tq, (h + 1) * tq)
        qis_ref[:, cols] = qit_ref[h * IDX_DIM:(h + 1) * IDX_DIM, :]
        qas_ref[0:A_HEAD_DIM, cols] = qa[h * A_HEAD_DIM:(h + 1) * A_HEAD_DIM, :]
        qas_ref[A_HEAD_DIM:A_HEAD_DIM + BF16_SUBLANES, cols] = jnp.where(first, -bound[h:h + 1, :], 0.0).astype(BF16)

    q0 = qb * tq
    kd = q0 // ck
    nkc = kd + 1
    w = wit_ref[...] * idx_scale

    def store_planes(kc, sc):
        bits = lax.bitcast_convert_type(sc, jnp.int32)
        ukey = bits ^ ((bits >> 31) | jnp.int32(-2 ** 31))
        for j in range(blocks_per_chunk):
            rows = [ukey[(j * KEY_BITS + e) * F32_SUBLANES:(j * KEY_BITS + e + 1) * F32_SUBLANES, :]
                    for e in range(KEY_BITS)]
            planes = _bit_transpose(rows)
            for b in range(KEY_BITS):
                planes_ref[b, kc * blocks_per_chunk + j] = planes[b]

    def chunk_logits(kc):
        kt = ki_ref[pl.ds(pl.multiple_of(kc * ck, ck), ck), :]
        return jnp.dot(kt, qis_ref[...], preferred_element_type=F32)

    def chunk_scores(lg):
        sc = jnp.zeros((ck, tq), F32)
        for h in range(nh):
            sc = sc + w[h:h + 1, :] * jnp.maximum(lg[:, h * tq:(h + 1) * tq], 0.0)
        return sc

    def score_pair(j, carry):
        lg0 = chunk_logits(2 * j)
        lg1 = chunk_logits(2 * j + 1)
        sc0 = chunk_scores(lg0)
        sc_ref[2 * j] = sc0
        store_planes(2 * j, sc0)
        sc1 = chunk_scores(lg1)
        sc_ref[2 * j + 1] = sc1
        store_planes(2 * j + 1, sc1)
        return carry

    lax.fori_loop(0, kd // 2, score_pair, 0)

    @pl.when(kd % 2 == 1)
    def _():
        sc_last = chunk_scores(chunk_logits(kd - 1))
        sc_ref[kd - 1] = sc_last
        store_planes(kd - 1, sc_last)

    sc = chunk_scores(chunk_logits(kd))
    kpos = kd * ck + lax.broadcasted_iota(jnp.int32, (ck, tq), 0)
    qpos = q0 + lax.broadcasted_iota(jnp.int32, (ck, tq), 1)
    sc = jnp.where(kpos <= qpos, sc, -jnp.inf)
    sc_ref[kd] = sc
    store_planes(kd, sc)

    live_blocks = nkc * blocks_per_chunk
    for blk in range(n_blocks):
        cand_ref[blk] = jnp.full((F32_SUBLANES, tq), jnp.where(blk < live_blocks, -1, 0), jnp.int32)

    def select_body(i, carry):
        need, ukey_thr = carry
        accs = [jnp.zeros((F32_SUBLANES, tq), jnp.int32) for _ in range(4)]
        for blk in range(n_blocks):
            accs[blk % 4] = accs[blk % 4] + lax.population_count(cand_ref[blk] & planes_ref[i, blk])
        acc = (accs[0] + accs[1]) + (accs[2] + accs[3])
        ones_cnt = jnp.sum(acc, axis=0, keepdims=True)
        take = ones_cnt >= need
        flip = jnp.where(take, 0, -1)
        for blk in range(n_blocks):
            cand_ref[blk] = cand_ref[blk] & (planes_ref[i, blk] ^ flip)
        need = jnp.where(take, need, need - ones_cnt)
        ukey_thr = ukey_thr | jnp.where(take, lax.shift_left(jnp.int32(1), KEY_BITS - 1 - i), 0)
        return need, ukey_thr

    init = (jnp.full((1, tq), top_k, jnp.int32), jnp.zeros((1, tq), jnp.int32))
    need, ukey_thr = lax.fori_loop(0, KEY_BITS, select_body, init)
    thr_bits = jnp.where(ukey_thr < 0, ukey_thr ^ jnp.int32(-2 ** 31), ~ukey_thr)
    thr = lax.bitcast_convert_type(thr_bits, F32)
    qrow = q0 + lax.broadcasted_iota(jnp.int32, (1, tq), 1)
    thr = jnp.where(qrow < top_k, jnp.finfo(F32).min, thr)

    tied = jnp.zeros((F32_SUBLANES, tq), jnp.int32)
    for blk in range(n_blocks):
        tied = tied + lax.population_count(cand_ref[blk])
    surplus = (jnp.sum(tied, axis=0, keepdims=True) > need) & (qrow >= top_k)

    @pl.when(jnp.max(jnp.where(surplus, 1, 0)) > 0)
    def _():
        key_row = lax.broadcasted_iota(jnp.int32, (ck, ck), 0)
        key_col = lax.broadcasted_iota(jnp.int32, (ck, ck), 1)
        upto = jnp.where(key_col <= key_row, 1.0, 0.0).astype(BF16)
        need_f = need.astype(F32)

        def drop_body(kc, seen):
            tie = jnp.where((sc_ref[kc] == thr) & surplus, 1.0, 0.0)
            count = jnp.sum(tie, axis=0, keepdims=True)

            @pl.when(jnp.max(count) > 0.0)
            def _():
                rank = jnp.dot(upto, tie.astype(BF16), preferred_element_type=F32) + seen
                sc_ref[kc] = jnp.where((tie > 0.0) & (rank > need_f), -jnp.inf, sc_ref[kc])

            return seen + count

        lax.fori_loop(0, nkc, drop_body, jnp.zeros((1, tq), F32))

    acc_ref[...] = jnp.zeros(acc_ref.shape, F32)

    def fast_body(kc, carry):
        off = pl.multiple_of(kc * ck, ck)
        s = jnp.dot(kaug_ref[pl.ds(off, ck), :], qas_ref[...], preferred_element_type=F32)
        neg = jnp.where(sc_ref[kc] >= thr, 0.0, -jnp.inf)
        for h in range(nh):
            cols = slice(h * tq, (h + 1) * tq)
            p_ref[:, cols] = jnp.exp2(s[:, cols] + neg).astype(BF16)
        pv = jnp.dot(vext_ref[kc], p_ref[...], preferred_element_type=F32)
        for h in range(nh):
            acc_ref[h] = acc_ref[h] + pv[:, h * tq:(h + 1) * tq]
        return carry

    lax.fori_loop(0, nkc, fast_body, 0)

    denom = acc_ref[0, A_KV_DIM:A_KV_DIM + F32_SUBLANES, :]
    for h in range(1, nh):
        denom = jnp.minimum(denom, acc_ref[h, A_KV_DIM:A_KV_DIM + F32_SUBLANES, :])
    healthy = jnp.min(denom) >= 2.0 ** -100

    @pl.when(jnp.logical_not(healthy))
    def _():
        m_ref[...] = jnp.full(m_ref.shape, jnp.finfo(F32).min, F32)
        acc_ref[...] = jnp.zeros(acc_ref.shape, F32)

        def attn_body(kc, carry):
            off = pl.multiple_of(kc * ck, ck)
            s = jnp.dot(ka_ref[pl.ds(off, ck), :], qas_ref[0:A_HEAD_DIM, :], preferred_element_type=F32)
            neg = jnp.where(sc_ref[kc] >= thr, 0.0, -jnp.inf)
            for h in range(nh):
                cols = slice(h * tq, (h + 1) * tq)
                sh = s[:, cols] + neg
                cm = jnp.max(sh.reshape(ck // MAX_ROWS, MAX_ROWS, tq), axis=0)
                m_old = m_ref[h:h + 1, :]
                m_new = jnp.maximum(m_old, jnp.max(cm, axis=0, keepdims=True))
                alpha_ref[h:h + 1, :] = jnp.exp2(m_old - m_new)
                m_ref[h:h + 1, :] = m_new
                p_ref[:, cols] = jnp.exp2(sh - m_new).astype(BF16)
            pv = jnp.dot(vext_ref[kc], p_ref[...], preferred_element_type=F32)
            for h in range(nh):
                acc_ref[h] = alpha_ref[h:h + 1, :] * acc_ref[h] + pv[:, h * tq:(h + 1) * tq]
            return carry

        lax.fori_loop(0, nkc, attn_body, 0)

    o_t = []
    for h in range(nh):
        a = acc_ref[h]
        o_t.append(a[0:A_KV_DIM, :] / a[A_KV_DIM:A_KV_DIM + 1, :])
    o = jnp.concatenate(o_t, axis=0).T
    g = ga_ref[...]
    o_ref[...] = (o * (g * _sigmoid(g))).astype(o_ref.dtype)


def _dsa(qit, wit, qat, ga, ki, ka, vat, B, L):
    tq, ck = DSA_Q_COLS, min(DSA_KEY_CHUNK, L)
    top_k = min(TOPK_MAX, L // 4)
    nq = L // tq
    nck = L // ck
    assert L % ck == 0 and ck % BLOCK_KEYS == 0
    col = lambda b, q: (0, b * nq + q)
    kernel = functools.partial(_dsa_kernel, tq=tq, ck=ck, top_k=top_k)
    return pl.pallas_call(
        kernel,
        grid=(B, nq),
        in_specs=[pl.BlockSpec((A_WIDTH, tq), col),
                  pl.BlockSpec((IDX_HEADS, tq), col),
                  pl.BlockSpec((A_WIDTH, tq), col),
                  pl.BlockSpec((tq, A_WIDTH), lambda b, q: (b * nq + q, 0)),
                  pl.BlockSpec((L, IDX_DIM), lambda b, q: (b, 0)),
                  pl.BlockSpec((L, A_KV_DIM), lambda b, q: (b, 0)),
                  pl.BlockSpec((A_KV_DIM, L), lambda b, q: (0, b))],
        out_specs=pl.BlockSpec((tq, A_WIDTH), lambda b, q: (b * nq + q, 0)),
        out_shape=jax.ShapeDtypeStruct((B * L, A_WIDTH), BF16),
        scratch_shapes=[pltpu.VMEM((IDX_DIM, A_HEADS * tq), BF16),
                        pltpu.VMEM((LANES, A_HEADS * tq), BF16),
                        pltpu.VMEM((L, LANES), BF16),
                        pltpu.VMEM((1, 1), F32),
                        pltpu.VMEM((nck, DSA_VEXT_ROWS, ck), BF16),
                        pltpu.VMEM((nck, ck, tq), F32),
                        pltpu.VMEM((KEY_BITS, L // BLOCK_KEYS, F32_SUBLANES, tq), jnp.int32),
                        pltpu.VMEM((L // BLOCK_KEYS, F32_SUBLANES, tq), jnp.int32),
                        pltpu.VMEM((A_HEADS, tq), F32),
                        pltpu.VMEM((A_HEADS, tq), F32),
                        pltpu.VMEM((A_HEADS, DSA_VEXT_ROWS, tq), F32),
                        pltpu.VMEM((ck, A_HEADS * tq), BF16)],
        compiler_params=_params("arbitrary", "arbitrary"),
        name="dsa",
    )(qit, wit, qat, ga, ki, ka, vat)


def _gla_kernel(q_ref, k_ref, v_ref, gb_ref, small_ref, wup_ref, bg_ref, gh_ref, o_ref, s_ref,
                *, c, sub, band):
    @pl.when(pl.program_id(1) == 0)
    def _():
        s_ref[...] = jnp.zeros(s_ref.shape, F32)

    hi = lax.Precision.HIGHEST
    row = lax.broadcasted_iota(jnp.int32, (c, c), 0)
    col = lax.broadcasted_iota(jnp.int32, (c, c), 1)
    tri = jnp.where(col <= row, 1.0, 0.0)
    lane = lax.broadcasted_iota(jnp.int32, (B_KEY_DIM, c), 1)
    krow = lax.broadcasted_iota(jnp.int32, (c, B_KEY_DIM), 0)
    diff = col - row
    ones = jnp.ones((B_KEY_DIM, c), BF16)
    small = small_ref[...]
    halves = []
    half = sub // 2
    while half >= band:
        halves.append((half, (row // (2 * half)) == (col // (2 * half))))
        half //= 2

    heads = range(B_HEADS)
    kcols = [slice(h * B_KEY_DIM, (h + 1) * B_KEY_DIM) for h in heads]
    vcols = [slice(h * B_VAL_DIM, (h + 1) * B_VAL_DIM) for h in heads]
    q = [q_ref[:, kcols[h]] * (B_KEY_DIM ** -0.5) for h in heads]
    k = [k_ref[:, kcols[h]] for h in heads]
    v = [v_ref[:, vcols[h]] for h in heads]
    xg = [jnp.dot(small, wup_ref[:, kcols[h]], precision=hi, preferred_element_type=F32) + bg_ref[:, kcols[h]]
          for h in heads]
    la = [(jnp.minimum(x, 0.0) - jnp.log(1.0 + jnp.exp(-jnp.abs(x)))) * (1.0 / GATE_TAU) for x in xg]
    g = [jnp.dot(tri, l, precision=hi, preferred_element_type=F32) for l in la]
    gt = [x.T for x in g]
    kt = [x.T for x in k]

    o = [jnp.dot((q[h] * jnp.exp(g[h])).astype(BF16), s_ref[h].astype(BF16), preferred_element_type=F32)
         for h in heads]

    a_rows = [[jnp.zeros((sub, c), F32)] for h in heads]
    for i in range(1, c // sub):
        lo = i * sub
        for h in heads:
            expo = jnp.where(lane < lo, gt[h][:, lo - 1:lo] - gt[h], -jnp.inf)
            kti = (kt[h] * jnp.exp(expo)).astype(BF16)
            qi = (q[h][lo:lo + sub, :] * jnp.exp(g[h][lo:lo + sub, :] - g[h][lo - 1:lo, :])).astype(BF16)
            a_rows[h].append(jnp.dot(qi, kti, preferred_element_type=F32))
    a = [jnp.concatenate(a_rows[h], axis=0) for h in heads]

    for half, same_block in halves:
        size = 2 * half
        pos = krow % size
        for h in heads:
            g3 = g[h].reshape(c // size, size, B_KEY_DIM)
            ref = jnp.broadcast_to(g3[:, half - 1:half, :], g3.shape).reshape(c, B_KEY_DIM)
            ql = (q[h] * jnp.exp(jnp.where(pos >= half, g[h] - ref, -jnp.inf))).astype(BF16)
            kl = (k[h] * jnp.exp(jnp.where(pos < half, ref - g[h], -jnp.inf))).astype(BF16)
            al = lax.dot_general(ql, kl, NT_DIMS, preferred_element_type=F32)
            a[h] = a[h] + jnp.where(same_block, al, 0.0)

    rmod = krow % band
    for d in range(band):
        for h in heads:
            ks = k[h] if d == 0 else pltpu.roll(k[h], d, axis=0)
            gs = g[h] if d == 0 else pltpu.roll(g[h], d, axis=0)
            e = jnp.where(rmod >= d, g[h] - gs, -jnp.inf)
            r = jnp.dot((q[h] * ks * jnp.exp(e)).astype(BF16), ones, preferred_element_type=F32)
            a[h] = a[h] + jnp.where(diff == -d, r, 0.0)

    for h in heads:
        oh = o[h] + jnp.dot(a[h].astype(BF16), v[h], preferred_element_type=F32)
        glast = gt[h][:, c - 1:c]
        kend = (kt[h] * jnp.exp(glast - gt[h])).astype(BF16)
        s_ref[h] = jnp.exp(glast) * s_ref[h] + jnp.dot(kend, v[h], preferred_element_type=F32)
        gb = gb_ref[:, vcols[h]]
        o_ref[:, vcols[h]] = (_rms(oh, gh_ref[...]) * (gb * _sigmoid(gb))).astype(o_ref.dtype)


def _gla(qb, kb, vb, gb, small, wup_pad, b_gate, g_head, B, L):
    c = GLA_CHUNK
    nc = L // c
    row = lambda b, i: (b * nc + i, 0)
    fixed = lambda b, i: (0, 0)
    return pl.pallas_call(
        functools.partial(_gla_kernel, c=c, sub=GLA_SUB, band=GLA_BAND),
        grid=(B, nc),
        in_specs=[pl.BlockSpec((c, B_KEY_WIDTH), row),
                  pl.BlockSpec((c, B_KEY_WIDTH), row),
                  pl.BlockSpec((c, B_VAL_WIDTH), row),
                  pl.BlockSpec((c, B_VAL_WIDTH), row),
                  pl.BlockSpec((c, LANES), row),
                  pl.BlockSpec((LANES, B_KEY_WIDTH), fixed),
                  pl.BlockSpec((1, B_KEY_WIDTH), fixed),
                  pl.BlockSpec((1, B_VAL_DIM), fixed)],
        out_specs=pl.BlockSpec((c, B_VAL_WIDTH), row),
        out_shape=jax.ShapeDtypeStruct((B * L, B_VAL_WIDTH), BF16),
        scratch_shapes=[pltpu.VMEM((B_HEADS, B_KEY_DIM, B_VAL_DIM), F32)],
        compiler_params=_params("arbitrary", "arbitrary"),
        name="gla",
    )(qb, kb, vb, gb, small, wup_pad, b_gate.reshape(1, B_KEY_WIDTH), g_head.reshape(1, B_VAL_DIM))


def _merge_kernel(x_ref, p_ref, oa_ref, ob_ref, ma_ref, mb_ref, wpa_ref, wpb_ref, wout_ref,
                  gpost_ref, wple_ref, wpg_ref, gpre_ref, gpost2_ref, o_ref, *, parts):
    rows = x_ref.shape[0] // parts
    sl = [slice(i * rows, (i + 1) * rows) for i in range(parts)]
    dot = lambda a, w_ref: jnp.dot(a, w_ref[...], preferred_element_type=F32)
    ya = [dot(oa_ref[r, :], wpa_ref) for r in sl]
    yb = [dot(ob_ref[r, :], wpb_ref) for r in sl]
    y = [_sigmoid(ma_ref[r, :]) * a + _sigmoid(mb_ref[r, :]) * b for r, a, b in zip(sl, ya, yb)]
    u = [dot(t.astype(BF16), wout_ref) for t in y]
    x1 = [x_ref[r, :] + _rms(t, gpost_ref[...]) for r, t in zip(sl, u)]
    gate = [dot(_rms(t, gpre_ref[...]).astype(BF16), wpg_ref) for t in x1]
    e = [dot(p_ref[r, :].astype(BF16), wple_ref) * _sigmoid(t) for r, t in zip(sl, gate)]
    for r, a, b in zip(sl, x1, e):
        o_ref[r, :] = a + _rms(b, gpost2_ref[...])


def _merge(x2d, p2d, oa, ob, ma, mb, wpa, wpb, wout, g_post, wple, wpg, g_pre2, g_post2):
    T = x2d.shape[0]
    tm = MERGE_ROWS
    rows = lambda width: pl.BlockSpec((tm, width), lambda i: (i, 0))
    full = lambda a: pl.BlockSpec(a.shape, lambda i: (0, 0))
    vec = lambda a: a.reshape(1, D_MODEL)
    args = (x2d, p2d, oa, ob, ma, mb, wpa, wpb, wout, vec(g_post), wple, wpg, vec(g_pre2), vec(g_post2))
    in_specs = [rows(D_MODEL), rows(PLE_DIM), rows(A_WIDTH), rows(B_VAL_WIDTH), rows(D_MODEL), rows(D_MODEL)]
    in_specs += [full(a) for a in args[6:]]
    return pl.pallas_call(
        functools.partial(_merge_kernel, parts=MERGE_PARTS),
        grid=(T // tm,),
        in_specs=in_specs,
        out_specs=rows(D_MODEL),
        out_shape=jax.ShapeDtypeStruct((T, D_MODEL), F32),
        compiler_params=_params("arbitrary"),
        name="merge",
    )(*args)


def _split_w_in(w_in):
    offs = [0]
    for s in IN_SPLITS:
        offs.append(offs[-1] + s)
    col = lambda i: w_in[:, offs[i]:offs[i + 1]]
    qa, ka, va, qi, ki, wi, ga, qb, kb, vb, gdown, gb, ma, mb = [col(i) for i in range(len(IN_SPLITS))]
    pad = jnp.zeros((D_MODEL, LANES - GATE_RANK), w_in.dtype)
    small = jnp.concatenate([gdown, pad], axis=1)
    groups = [qa.T, ka, va.T, qi.T, ki, wi.T, small, ga, qb, kb, vb, gb, ma, mb]
    dtypes = [BF16, BF16, BF16, BF16, BF16, F32, F32, F32, F32, F32, BF16, F32, F32, F32]
    transposed = [True, False, True, True, False, True] + [False] * 8
    return [g.astype(BF16) for g in groups], dtypes, transposed


def _layer(x2d, p2d, B, L, g_pre, w_in, w_gate_up, b_gate, g_gla_head, w_proj_a, w_proj_b,
           w_out, g_post, w_ple, w_ple_gate, g_ple_pre, g_ple_post):
    groups, dtypes, transposed = _split_w_in(w_in)
    qat, ka, vat, qit, ki, wit, small, ga, qb, kb, vb, gb, ma, mb = _in_proj(
        x2d, g_pre, groups, dtypes, transposed)
    oa = _dsa(qit, wit, qat, ga, ki, ka, vat, B, L)
    wup_pad = jnp.zeros((LANES, B_KEY_WIDTH), F32).at[SMALL_GD_OFF:SMALL_GD_OFF + GATE_RANK].set(w_gate_up)
    ob = _gla(qb, kb, vb, gb, small, wup_pad, b_gate, g_gla_head, B, L)
    bf = lambda a: a.astype(BF16)
    return _merge(x2d, p2d, oa, ob, ma, mb, bf(w_proj_a), bf(w_proj_b), bf(w_out), g_post,
                  bf(w_ple), bf(w_ple_gate), g_ple_pre, g_ple_post)


def kernel(x, p, g_pre, w_in, w_gate_up, b_gate, g_gla_head, w_proj_a, w_proj_b, w_out, g_post,
           w_ple, w_ple_gate, g_ple_pre, g_ple_post):
    B, L, _ = x.shape
    depth = p.shape[0]
    x2d = x.reshape(B * L, D_MODEL)
    for i in range(depth):
        x2d = _layer(x2d, p[i].reshape(B * L, PLE_DIM), B, L, g_pre[i], w_in[i], w_gate_up[i],
                     b_gate[i], g_gla_head[i], w_proj_a[i], w_proj_b[i], w_out[i], g_post[i],
                     w_ple[i], w_ple_gate[i], g_ple_pre[i], g_ple_post[i])
    return x2d.reshape(B, L, D_MODEL)
```

```python
import functools

import jax
import jax.numpy as jnp
from jax import lax
from jax.experimental import pallas as pl
from jax.experimental.pallas import tpu as pltpu

F32 = jnp.float32
BF16 = jnp.bfloat16

D_MODEL = 1024
PLE_DIM = 256
A_HEADS = 8
A_HEAD_DIM = 64
A_WIDTH = A_HEADS * A_HEAD_DIM
A_KV_DIM = 64
IDX_HEADS = 8
IDX_DIM = 64
TOPK_MAX = 256
B_HEADS = 4
B_KEY_DIM = 128
B_VAL_DIM = 256
B_KEY_WIDTH = B_HEADS * B_KEY_DIM
B_VAL_WIDTH = B_HEADS * B_VAL_DIM
GATE_RANK = 16
GATE_TAU = 16.0
EPS = 1e-6

IN_SPLITS = (A_WIDTH, A_KV_DIM, A_KV_DIM, IDX_HEADS * IDX_DIM, IDX_DIM, IDX_HEADS, A_WIDTH,
             B_KEY_WIDTH, B_KEY_WIDTH, B_VAL_WIDTH, GATE_RANK, B_VAL_WIDTH,
             D_MODEL, D_MODEL)

VMEM_LIMIT_BYTES = 56 * 1024 * 1024
LANES = 128
BF16_SUBLANES = 16

SMALL_GD_OFF = 0

IN_PROJ_ROWS = 256
MERGE_ROWS = 512
MERGE_PARTS = 2
DSA_Q_COLS = 256
DSA_KEY_CHUNK = 512
DSA_VEXT_ROWS = A_KV_DIM + BF16_SUBLANES
GLA_CHUNK = 128
GLA_SUB = 16
GLA_BAND = 4

NT_DIMS = (((1,), (1,)), ((), ()))


def _sigmoid(x):
    return 1.0 / (1.0 + jnp.exp(-x))


def _rms(x, g):
    ms = jnp.mean(x * x, axis=-1, keepdims=True)
    return x * lax.rsqrt(ms + EPS) * g


def _params(*sem):
    return pltpu.CompilerParams(dimension_semantics=sem, vmem_limit_bytes=VMEM_LIMIT_BYTES)


def _in_proj_kernel(x_ref, g_ref, *refs, transposed):
    n_out = len(transposed)
    w_refs, o_refs = refs[:n_out], refs[n_out:]
    h = _rms(x_ref[...], g_ref[...]).astype(BF16)
    for w_ref, o_ref, tr in zip(w_refs, o_refs, transposed):
        if tr:
            z = lax.dot_general(w_ref[...], h, NT_DIMS, preferred_element_type=F32)
        else:
            z = jnp.dot(h, w_ref[...], preferred_element_type=F32)
        o_ref[...] = z.astype(o_ref.dtype)


def _in_proj(x2d, g_pre, w_groups, out_dtypes, transposed):
    T = x2d.shape[0]
    tm = IN_PROJ_ROWS
    in_specs = [pl.BlockSpec((tm, D_MODEL), lambda i: (i, 0)),
                pl.BlockSpec((1, D_MODEL), lambda i: (0, 0))]
    in_specs += [pl.BlockSpec(w.shape, lambda i: (0, 0)) for w in w_groups]
    out_specs, out_shape = [], []
    for w, dt, tr in zip(w_groups, out_dtypes, transposed):
        if tr:
            out_specs.append(pl.BlockSpec((w.shape[0], tm), lambda i: (0, i)))
            out_shape.append(jax.ShapeDtypeStruct((w.shape[0], T), dt))
        else:
            out_specs.append(pl.BlockSpec((tm, w.shape[1]), lambda i: (i, 0)))
            out_shape.append(jax.ShapeDtypeStruct((T, w.shape[1]), dt))
    return pl.pallas_call(
        functools.partial(_in_proj_kernel, transposed=tuple(transposed)),
        grid=(T // tm,),
        in_specs=in_specs,
        out_specs=out_specs,
        out_shape=out_shape,
        compiler_params=_params("arbitrary"),
        name="in_proj",
    )(x2d, g_pre.reshape(1, D_MODEL), *w_groups)


LOG2E = 1.4426950408889634
MAX_ROWS = 4 * 8
F32_SUBLANES = 8
KEY_BITS = 32
BLOCK_KEYS = KEY_BITS * F32_SUBLANES
SELECT_VARIANTS = 4
MIN_HEALTHY_DENOM = 2.0 ** -100


def _bit_transpose(words):
    a = list(words)
    j, m = 16, 0x0000FFFF
    while j:
        mask = jnp.int32(m if m < 2 ** 31 else m - 2 ** 32)
        k = 0
        while k < KEY_BITS:
            t = (a[k] ^ lax.shift_right_logical(a[k + j], jnp.int32(j))) & mask
            a[k] = a[k] ^ t
            a[k + j] = a[k + j] ^ lax.shift_left(t, jnp.int32(j))
            k = (k + j + 1) & ~j
        j >>= 1
        m = (m ^ (m << j)) & 0xFFFFFFFF
    return a


def _dsa_kernel(qit_ref, wit_ref, qat_ref, ga_ref, ki_ref, ka_ref, vat_ref, o_ref,
                qis_ref, qas_ref, kaug_ref, knorm_ref, vext_ref, sc_ref, planes_ref, cand_ref, sel_ref, ties_ref,
                m_ref, alpha_ref, acc_ref, p_ref, *, tq, ck, top_k):
    qb = pl.program_id(1)
    nh = A_HEADS
    idx_scale = (IDX_HEADS ** -0.5) * (IDX_DIM ** -0.5)
    attn_scale = (A_HEAD_DIM ** -0.5) * LOG2E
    n_chunks = vext_ref.shape[0]
    n_blocks = cand_ref.shape[0]
    blocks_per_chunk = ck // BLOCK_KEYS

    @pl.when(qb == 0)
    def _():
        for j in range(n_chunks):
            vext_ref[j, 0:A_KV_DIM, :] = vat_ref[:, j * ck:(j + 1) * ck]
            vext_ref[j, A_KV_DIM:, :] = jnp.ones((DSA_VEXT_ROWS - A_KV_DIM, ck), BF16)
        planes_ref[...] = jnp.zeros(planes_ref.shape, jnp.int32)
        kaug_ref[:, 0:A_KV_DIM] = ka_ref[...]
        one_col = lax.broadcasted_iota(jnp.int32, (kaug_ref.shape[0], LANES - A_KV_DIM), 1) == 0
        kaug_ref[:, A_KV_DIM:] = jnp.where(one_col, 1.0, 0.0).astype(BF16)
        qas_ref[A_HEAD_DIM + BF16_SUBLANES:, :] = jnp.zeros((LANES - A_HEAD_DIM - BF16_SUBLANES, nh * tq), BF16)
        k32 = ka_ref[...].astype(F32)
        knorm_ref[...] = jnp.sqrt(jnp.max(jnp.sum(k32 * k32, axis=1, keepdims=True), axis=0, keepdims=True))

    qa = (qat_ref[...].astype(F32) * attn_scale).astype(BF16)
    q32 = qa.astype(F32).reshape(nh, A_HEAD_DIM, tq)
    bound = jnp.sqrt(jnp.sum(q32 * q32, axis=1)) * knorm_ref[...]
    first = lax.broadcasted_iota(jnp.int32, (BF16_SUBLANES, tq), 0) == 0
    for h in range(nh):
        cols = slice(h * tq, (h + 1) * tq)
        qis_ref[:, cols] = qit_ref[h * IDX_DIM:(h + 1) * IDX_DIM, :]
        qas_ref[0:A_HEAD_DIM, cols] = qa[h * A_HEAD_DIM:(h + 1) * A_HEAD_DIM, :]
        qas_ref[A_HEAD_DIM:A_HEAD_DIM + BF16_SUBLANES, cols] = jnp.where(first, -bound[h:h + 1, :], 0.0).astype(BF16)

    q0 = qb * tq
    kd = q0 // ck
    nkc = kd + 1
    w = wit_ref[...] * idx_scale

    def store_planes(kc, sc):
        bits = lax.bitcast_convert_type(sc, jnp.int32)
        ukey = bits ^ ((bits >> 31) | jnp.int32(-2 ** 31))
        for j in range(blocks_per_chunk):
            rows = [ukey[(j * KEY_BITS + e) * F32_SUBLANES:(j * KEY_BITS + e + 1) * F32_SUBLANES, :]
                    for e in range(KEY_BITS)]
            planes = _bit_transpose(rows)
            for b in range(KEY_BITS):
                planes_ref[b, kc * blocks_per_chunk + j] = planes[b]

    def chunk_logits(kc):
        kt = ki_ref[pl.ds(pl.multiple_of(kc * ck, ck), ck), :]
        return jnp.dot(kt, qis_ref[...], preferred_element_type=F32)

    def chunk_scores(lg):
        sc = jnp.zeros((ck, tq), F32)
        for h in range(nh):
            sc = sc + w[h:h + 1, :] * jnp.maximum(lg[:, h * tq:(h + 1) * tq], 0.0)
        return sc

    def score_body(kc, carry):
        sc_kc = chunk_scores(chunk_logits(kc))
        sc_ref[kc] = sc_kc
        store_planes(kc, sc_kc)
        return carry

    lax.fori_loop(0, kd, score_body, 0)

    sc = chunk_scores(chunk_logits(kd))
    kpos = kd * ck + lax.broadcasted_iota(jnp.int32, (ck, tq), 0)
    qpos = q0 + lax.broadcasted_iota(jnp.int32, (ck, tq), 1)
    sc = jnp.where(kpos <= qpos, sc, -jnp.inf)
    sc_ref[kd] = sc
    store_planes(kd, sc)

    live_blocks = nkc * blocks_per_chunk
    for blk in range(n_blocks):
        cand_ref[blk] = jnp.full((F32_SUBLANES, tq), jnp.where(blk < live_blocks, -1, 0), jnp.int32)

    def select_body(i, carry, blocks):
        need, ukey_thr = carry
        accs = [jnp.zeros((F32_SUBLANES, tq), jnp.int32) for _ in range(4)]
        for blk in range(blocks):
            accs[blk % 4] = accs[blk % 4] + lax.population_count(cand_ref[blk] & planes_ref[i, blk])
        acc = (accs[0] + accs[1]) + (accs[2] + accs[3])
        ones_cnt = jnp.sum(acc, axis=0, keepdims=True)
        take = ones_cnt >= need
        flip = jnp.where(take, 0, -1)
        for blk in range(blocks):
            cand_ref[blk] = cand_ref[blk] & (planes_ref[i, blk] ^ flip)
        need = jnp.where(take, need, need - ones_cnt)
        ukey_thr = ukey_thr | jnp.where(take, lax.shift_left(jnp.int32(1), KEY_BITS - 1 - i), 0)
        return need, ukey_thr

    init = (jnp.full((1, tq), top_k, jnp.int32), jnp.zeros((1, tq), jnp.int32))
    covered = 0
    for blocks in range(n_blocks // SELECT_VARIANTS, n_blocks + 1, n_blocks // SELECT_VARIANTS):
        @pl.when((live_blocks > covered) & (live_blocks <= blocks))
        def _(blocks=blocks):
            found = lax.fori_loop(0, KEY_BITS, functools.partial(select_body, blocks=blocks), init)
            sel_ref[0:1, :] = found[0]
            sel_ref[1:2, :] = found[1]
        covered = blocks
    need = sel_ref[0:1, :]
    ukey_thr = sel_ref[1:2, :]
    thr_bits = jnp.where(ukey_thr < 0, ukey_thr ^ jnp.int32(-2 ** 31), ~ukey_thr)
    thr = lax.bitcast_convert_type(thr_bits, F32)
    qrow = q0 + lax.broadcasted_iota(jnp.int32, (1, tq), 1)
    thr = jnp.where(qrow < top_k, jnp.finfo(F32).min, thr)

    tied = jnp.zeros((F32_SUBLANES, tq), jnp.int32)
    for blk in range(n_blocks):
        tied = tied + lax.population_count(cand_ref[blk])
    surplus = (jnp.sum(tied, axis=0, keepdims=True) > need) & (qrow >= top_k)

    @pl.when(jnp.max(jnp.where(surplus, 1, 0)) > 0)
    def _():
        surplus_bits = jnp.where(surplus, -1, 0)
        for c in range(n_chunks):
            cnt = jnp.zeros((F32_SUBLANES, tq), jnp.int32)
            for j in range(blocks_per_chunk):
                cnt = cnt + lax.population_count(cand_ref[c * blocks_per_chunk + j] & surplus_bits)
            ties_ref[c:c + 1, :] = jnp.sum(cnt, axis=0, keepdims=True)
        any_tie = jnp.max(ties_ref[...], axis=1, keepdims=True) > 0
        chunk_bit = lax.shift_left(1, lax.broadcasted_iota(jnp.int32, (n_chunks, 1), 0))
        chunk_mask = jnp.sum(jnp.where(any_tie, chunk_bit, 0))

        key_row = lax.broadcasted_iota(jnp.int32, (ck, ck), 0)
        key_col = lax.broadcasted_iota(jnp.int32, (ck, ck), 1)
        upto = jnp.where(key_col <= key_row, 1.0, 0.0).astype(BF16)
        need_f = need.astype(F32)

        def drop_body(kc, seen):
            @pl.when((lax.shift_right_logical(chunk_mask, kc) & 1) == 1)
            def _():
                sc = sc_ref[kc]
                tie = jnp.where((sc == thr) & surplus, 1.0, 0.0)
                rank = jnp.dot(upto, tie.astype(BF16), preferred_element_type=F32) + seen
                sc_ref[kc] = jnp.where((tie > 0.0) & (rank > need_f), -jnp.inf, sc)

            return seen + ties_ref[pl.ds(kc, 1), :].astype(F32)

        lax.fori_loop(0, nkc, drop_body, jnp.zeros((1, tq), F32))

    acc_ref[...] = jnp.zeros(acc_ref.shape, F32)

    def fast_body(kc, carry):
        off = pl.multiple_of(kc * ck, ck)
        s = jnp.dot(kaug_ref[pl.ds(off, ck), :], qas_ref[...], preferred_element_type=F32)
        neg = jnp.where(sc_ref[kc] >= thr, 0.0, -jnp.inf)
        for h in range(nh):
            cols = slice(h * tq, (h + 1) * tq)
            p_ref[:, cols] = jnp.exp2(s[:, cols] + neg).astype(BF16)
        pv = jnp.dot(vext_ref[kc], p_ref[...], preferred_element_type=F32)
        for h in range(nh):
            acc_ref[h] = acc_ref[h] + pv[:, h * tq:(h + 1) * tq]
        return carry

    lax.fori_loop(0, nkc, fast_body, 0)

    denom = acc_ref[0, A_KV_DIM:A_KV_DIM + F32_SUBLANES, :]
    for h in range(1, nh):
        denom = jnp.minimum(denom, acc_ref[h, A_KV_DIM:A_KV_DIM + F32_SUBLANES, :])
    healthy = jnp.min(denom) >= MIN_HEALTHY_DENOM

    @pl.when(jnp.logical_not(healthy))
    def _():
        m_ref[...] = jnp.full(m_ref.shape, jnp.finfo(F32).min, F32)
        acc_ref[...] = jnp.zeros(acc_ref.shape, F32)

        def attn_body(kc, carry):
            off = pl.multiple_of(kc * ck, ck)
            s = jnp.dot(ka_ref[pl.ds(off, ck), :], qas_ref[0:A_HEAD_DIM, :], preferred_element_type=F32)
            neg = jnp.where(sc_ref[kc] >= thr, 0.0, -jnp.inf)
            for h in range(nh):
                cols = slice(h * tq, (h + 1) * tq)
                sh = s[:, cols] + neg
                cm = jnp.max(sh.reshape(ck // MAX_ROWS, MAX_ROWS, tq), axis=0)
                m_old = m_ref[h:h + 1, :]
                m_new = jnp.maximum(m_old, jnp.max(cm, axis=0, keepdims=True))
                alpha_ref[h:h + 1, :] = jnp.exp2(m_old - m_new)
                m_ref[h:h + 1, :] = m_new
                p_ref[:, cols] = jnp.exp2(sh - m_new).astype(BF16)
            pv = jnp.dot(vext_ref[kc], p_ref[...], preferred_element_type=F32)
            for h in range(nh):
                acc_ref[h] = alpha_ref[h:h + 1, :] * acc_ref[h] + pv[:, h * tq:(h + 1) * tq]
            return carry

        lax.fori_loop(0, nkc, attn_body, 0)

    o_t = []
    for h in range(nh):
        a = acc_ref[h]
        o_t.append(a[0:A_KV_DIM, :] / a[A_KV_DIM:A_KV_DIM + 1, :])
    o = jnp.concatenate(o_t, axis=0).T
    g = ga_ref[...]
    o_ref[...] = (o * (g * _sigmoid(g))).astype(o_ref.dtype)


def _dsa(qit, wit, qat, ga, ki, ka, vat, B, L):
    tq, ck = DSA_Q_COLS, min(DSA_KEY_CHUNK, L)
    top_k = min(TOPK_MAX, L // 4)
    nq = L // tq
    nck = L // ck
    assert L % ck == 0 and ck % BLOCK_KEYS == 0 and (L // BLOCK_KEYS) % SELECT_VARIANTS == 0
    col = lambda b, q: (0, b * nq + q)
    kernel = functools.partial(_dsa_kernel, tq=tq, ck=ck, top_k=top_k)
    return pl.pallas_call(
        kernel,
        grid=(B, nq),
        in_specs=[pl.BlockSpec((A_WIDTH, tq), col),
                  pl.BlockSpec((IDX_HEADS, tq), col),
                  pl.BlockSpec((A_WIDTH, tq), col),
                  pl.BlockSpec((tq, A_WIDTH), lambda b, q: (b * nq + q, 0)),
                  pl.BlockSpec((L, IDX_DIM), lambda b, q: (b, 0)),
                  pl.BlockSpec((L, A_KV_DIM), lambda b, q: (b, 0)),
                  pl.BlockSpec((A_KV_DIM, L), lambda b, q: (0, b))],
        out_specs=pl.BlockSpec((tq, A_WIDTH), lambda b, q: (b * nq + q, 0)),
        out_shape=jax.ShapeDtypeStruct((B * L, A_WIDTH), BF16),
        scratch_shapes=[pltpu.VMEM((IDX_DIM, A_HEADS * tq), BF16),
                        pltpu.VMEM((LANES, A_HEADS * tq), BF16),
                        pltpu.VMEM((L, LANES), BF16),
                        pltpu.VMEM((1, 1), F32),
                        pltpu.VMEM((nck, DSA_VEXT_ROWS, ck), BF16),
                        pltpu.VMEM((nck, ck, tq), F32),
                        pltpu.VMEM((KEY_BITS, L // BLOCK_KEYS, F32_SUBLANES, tq), jnp.int32),
                        pltpu.VMEM((L // BLOCK_KEYS, F32_SUBLANES, tq), jnp.int32),
                        pltpu.VMEM((F32_SUBLANES, tq), jnp.int32),
                        pltpu.VMEM((nck, tq), jnp.int32),
                        pltpu.VMEM((A_HEADS, tq), F32),
                        pltpu.VMEM((A_HEADS, tq), F32),
                        pltpu.VMEM((A_HEADS, DSA_VEXT_ROWS, tq), F32),
                        pltpu.VMEM((ck, A_HEADS * tq), BF16)],
        compiler_params=_params("arbitrary", "arbitrary"),
        name="dsa",
    )(qit, wit, qat, ga, ki, ka, vat)


def _gla_kernel(q_ref, k_ref, v_ref, gb_ref, small_ref, wup_ref, bg_ref, gh_ref, o_ref, s_ref,
                *, c, sub, band):
    @pl.when(pl.program_id(1) == 0)
    def _():
        s_ref[...] = jnp.zeros(s_ref.shape, F32)

    hi = lax.Precision.HIGHEST
    row = lax.broadcasted_iota(jnp.int32, (c, c), 0)
    col = lax.broadcasted_iota(jnp.int32, (c, c), 1)
    tri = jnp.where(col <= row, 1.0, 0.0)
    lane = lax.broadcasted_iota(jnp.int32, (B_KEY_DIM, c), 1)
    krow = lax.broadcasted_iota(jnp.int32, (c, B_KEY_DIM), 0)
    diff = col - row
    ones = jnp.ones((B_KEY_DIM, c), BF16)
    small = small_ref[...]
    halves = []
    half = sub // 2
    while half >= band:
        halves.append((half, (row // (2 * half)) == (col // (2 * half))))
        half //= 2

    heads = range(B_HEADS)
    kcols = [slice(h * B_KEY_DIM, (h + 1) * B_KEY_DIM) for h in heads]
    vcols = [slice(h * B_VAL_DIM, (h + 1) * B_VAL_DIM) for h in heads]
    q = [q_ref[:, kcols[h]] * (B_KEY_DIM ** -0.5) for h in heads]
    k = [k_ref[:, kcols[h]] for h in heads]
    v = [v_ref[:, vcols[h]] for h in heads]
    xg = [jnp.dot(small, wup_ref[:, kcols[h]], precision=hi, preferred_element_type=F32) + bg_ref[:, kcols[h]]
          for h in heads]
    la = [(jnp.minimum(x, 0.0) - jnp.log(1.0 + jnp.exp(-jnp.abs(x)))) * (1.0 / GATE_TAU) for x in xg]
    g = [jnp.dot(tri, l, precision=hi, preferred_element_type=F32) for l in la]
    gt = [x.T for x in g]
    kt = [x.T for x in k]

    o = [jnp.dot((q[h] * jnp.exp(g[h])).astype(BF16), s_ref[h].astype(BF16), preferred_element_type=F32)
         for h in heads]

    a_rows = [[jnp.zeros((sub, c), F32)] for h in heads]
    for i in range(1, c // sub):
        lo = i * sub
        for h in heads:
            expo = jnp.where(lane < lo, gt[h][:, lo - 1:lo] - gt[h], -jnp.inf)
            kti = (kt[h] * jnp.exp(expo)).astype(BF16)
            qi = (q[h][lo:lo + sub, :] * jnp.exp(g[h][lo:lo + sub, :] - g[h][lo - 1:lo, :])).astype(BF16)
            a_rows[h].append(jnp.dot(qi, kti, preferred_element_type=F32))
    a = [jnp.concatenate(a_rows[h], axis=0) for h in heads]

    for half, same_block in halves:
        size = 2 * half
        pos = krow % size
        for h in heads:
            g3 = g[h].reshape(c // size, size, B_KEY_DIM)
            ref = jnp.broadcast_to(g3[:, half - 1:half, :], g3.shape).reshape(c, B_KEY_DIM)
            ql = (q[h] * jnp.exp(jnp.where(pos >= half, g[h] - ref, -jnp.inf))).astype(BF16)
            kl = (k[h] * jnp.exp(jnp.where(pos < half, ref - g[h], -jnp.inf))).astype(BF16)
            al = lax.dot_general(ql, kl, NT_DIMS, preferred_element_type=F32)
            a[h] = a[h] + jnp.where(same_block, al, 0.0)

    rmod = krow % band
    for d in range(band):
        for h in heads:
            ks = k[h] if d == 0 else pltpu.roll(k[h], d, axis=0)
            gs = g[h] if d == 0 else pltpu.roll(g[h], d, axis=0)
            e = jnp.where(rmod >= d, g[h] - gs, -jnp.inf)
            r = jnp.dot((q[h] * ks * jnp.exp(e)).astype(BF16), ones, preferred_element_type=F32)
            a[h] = a[h] + jnp.where(diff == -d, r, 0.0)

    for h in heads:
        oh = o[h] + jnp.dot(a[h].astype(BF16), v[h], preferred_element_type=F32)
        glast = gt[h][:, c - 1:c]
        kend = (kt[h] * jnp.exp(glast - gt[h])).astype(BF16)
        s_ref[h] = jnp.exp(glast) * s_ref[h] + jnp.dot(kend, v[h], preferred_element_type=F32)
        gb = gb_ref[:, vcols[h]]
        o_ref[:, vcols[h]] = (_rms(oh, gh_ref[...]) * (gb * _sigmoid(gb))).astype(o_ref.dtype)


def _gla(qb, kb, vb, gb, small, wup_pad, b_gate, g_head, B, L):
    c = GLA_CHUNK
    nc = L // c
    row = lambda b, i: (b * nc + i, 0)
    fixed = lambda b, i: (0, 0)
    return pl.pallas_call(
        functools.partial(_gla_kernel, c=c, sub=GLA_SUB, band=GLA_BAND),
        grid=(B, nc),
        in_specs=[pl.BlockSpec((c, B_KEY_WIDTH), row),
                  pl.BlockSpec((c, B_KEY_WIDTH), row),
                  pl.BlockSpec((c, B_VAL_WIDTH), row),
                  pl.BlockSpec((c, B_VAL_WIDTH), row),
                  pl.BlockSpec((c, LANES), row),
                  pl.BlockSpec((LANES, B_KEY_WIDTH), fixed),
                  pl.BlockSpec((1, B_KEY_WIDTH), fixed),
                  pl.BlockSpec((1, B_VAL_DIM), fixed)],
        out_specs=pl.BlockSpec((c, B_VAL_WIDTH), row),
        out_shape=jax.ShapeDtypeStruct((B * L, B_VAL_WIDTH), BF16),
        scratch_shapes=[pltpu.VMEM((B_HEADS, B_KEY_DIM, B_VAL_DIM), F32)],
        compiler_params=_params("arbitrary", "arbitrary"),
        name="gla",
    )(qb, kb, vb, gb, small, wup_pad, b_gate.reshape(1, B_KEY_WIDTH), g_head.reshape(1, B_VAL_DIM))


def _merge_kernel(x_ref, p_ref, oa_ref, ob_ref, ma_ref, mb_ref, wpa_ref, wpb_ref, wout_ref,
                  gpost_ref, wple_ref, wpg_ref, gpre_ref, gpost2_ref, o_ref, *, parts):
    rows = x_ref.shape[0] // parts
    sl = [slice(i * rows, (i + 1) * rows) for i in range(parts)]
    dot = lambda a, w_ref: jnp.dot(a, w_ref[...], preferred_element_type=F32)
    ya = [dot(oa_ref[r, :], wpa_ref) for r in sl]
    yb = [dot(ob_ref[r, :], wpb_ref) for r in sl]
    y = [_sigmoid(ma_ref[r, :]) * a + _sigmoid(mb_ref[r, :]) * b for r, a, b in zip(sl, ya, yb)]
    u = [dot(t.astype(BF16), wout_ref) for t in y]
    x1 = [x_ref[r, :] + _rms(t, gpost_ref[...]) for r, t in zip(sl, u)]
    gate = [dot(_rms(t, gpre_ref[...]).astype(BF16), wpg_ref) for t in x1]
    e = [dot(p_ref[r, :].astype(BF16), wple_ref) * _sigmoid(t) for r, t in zip(sl, gate)]
    for r, a, b in zip(sl, x1, e):
        o_ref[r, :] = a + _rms(b, gpost2_ref[...])


def _merge(x2d, p2d, oa, ob, ma, mb, wpa, wpb, wout, g_post, wple, wpg, g_pre2, g_post2):
    T = x2d.shape[0]
    tm = MERGE_ROWS
    rows = lambda width: pl.BlockSpec((tm, width), lambda i: (i, 0))
    full = lambda a: pl.BlockSpec(a.shape, lambda i: (0, 0))
    vec = lambda a: a.reshape(1, D_MODEL)
    args = (x2d, p2d, oa, ob, ma, mb, wpa, wpb, wout, vec(g_post), wple, wpg, vec(g_pre2), vec(g_post2))
    in_specs = [rows(D_MODEL), rows(PLE_DIM), rows(A_WIDTH), rows(B_VAL_WIDTH), rows(D_MODEL), rows(D_MODEL)]
    in_specs += [full(a) for a in args[6:]]
    return pl.pallas_call(
        functools.partial(_merge_kernel, parts=MERGE_PARTS),
        grid=(T // tm,),
        in_specs=in_specs,
        out_specs=rows(D_MODEL),
        out_shape=jax.ShapeDtypeStruct((T, D_MODEL), F32),
        compiler_params=_params("arbitrary"),
        name="merge",
    )(*args)


def _split_w_in(w_in):
    offs = [0]
    for s in IN_SPLITS:
        offs.append(offs[-1] + s)
    col = lambda i: w_in[:, offs[i]:offs[i + 1]]
    qa, ka, va, qi, ki, wi, ga, qb, kb, vb, gdown, gb, ma, mb = [col(i) for i in range(len(IN_SPLITS))]
    pad = jnp.zeros((D_MODEL, LANES - GATE_RANK), w_in.dtype)
    small = jnp.concatenate([gdown, pad], axis=1)
    groups = [qa.T, ka, va.T, qi.T, ki, wi.T, small, ga, qb, kb, vb, gb, ma, mb]
    dtypes = [BF16, BF16, BF16, BF16, BF16, F32, F32, F32, F32, F32, BF16, F32, F32, F32]
    transposed = [True, False, True, True, False, True] + [False] * 8
    return [g.astype(BF16) for g in groups], dtypes, transposed


def _layer(x2d, p2d, B, L, g_pre, w_in, w_gate_up, b_gate, g_gla_head, w_proj_a, w_proj_b,
           w_out, g_post, w_ple, w_ple_gate, g_ple_pre, g_ple_post):
    groups, dtypes, transposed = _split_w_in(w_in)
    qat, ka, vat, qit, ki, wit, small, ga, qb, kb, vb, gb, ma, mb = _in_proj(
        x2d, g_pre, groups, dtypes, transposed)
    oa = _dsa(qit, wit, qat, ga, ki, ka, vat, B, L)
    wup_pad = jnp.zeros((LANES, B_KEY_WIDTH), F32).at[SMALL_GD_OFF:SMALL_GD_OFF + GATE_RANK].set(w_gate_up)
    ob = _gla(qb, kb, vb, gb, small, wup_pad, b_gate, g_gla_head, B, L)
    bf = lambda a: a.astype(BF16)
    return _merge(x2d, p2d, oa, ob, ma, mb, bf(w_proj_a), bf(w_proj_b), bf(w_out), g_post,
                  bf(w_ple), bf(w_ple_gate), g_ple_pre, g_ple_post)


def kernel(x, p, g_pre, w_in, w_gate_up, b_gate, g_gla_head, w_proj_a, w_proj_b, w_out, g_post,
           w_ple, w_ple_gate, g_ple_pre, g_ple_post):
    B, L, _ = x.shape
    depth = p.shape[0]
    x2d = x.reshape(B * L, D_MODEL)
    for i in range(depth):
        x2d = _layer(x2d, p[i].reshape(B * L, PLE_DIM), B, L, g_pre[i], w_in[i], w_gate_up[i],
                     b_gate[i], g_gla_head[i], w_proj_a[i], w_proj_b[i], w_out[i], g_post[i],
                     w_ple[i], w_ple_gate[i], g_ple_pre[i], g_ple_post[i])
    return x2d.reshape(B, L, D_MODEL)
```

```python
import functools

import jax
import jax.numpy as jnp
from jax import lax
from jax.experimental import pallas as pl
from jax.experimental.pallas import tpu as pltpu

F32 = jnp.float32
BF16 = jnp.bfloat16

D_MODEL = 1024
PLE_DIM = 256
A_HEADS = 8
A_HEAD_DIM = 64
A_WIDTH = A_HEADS * A_HEAD_DIM
A_KV_DIM = 64
IDX_HEADS = 8
IDX_DIM = 64
TOPK_MAX = 256
B_HEADS = 4
B_KEY_DIM = 128
B_VAL_DIM = 256
B_KEY_WIDTH = B_HEADS * B_KEY_DIM
B_VAL_WIDTH = B_HEADS * B_VAL_DIM
GATE_RANK = 16
GATE_TAU = 16.0
EPS = 1e-6

IN_SPLITS = (A_WIDTH, A_KV_DIM, A_KV_DIM, IDX_HEADS * IDX_DIM, IDX_DIM, IDX_HEADS, A_WIDTH,
             B_KEY_WIDTH, B_KEY_WIDTH, B_VAL_WIDTH, GATE_RANK, B_VAL_WIDTH,
             D_MODEL, D_MODEL)

VMEM_LIMIT_BYTES = 56 * 1024 * 1024
LANES = 128
BF16_SUBLANES = 16

SMALL_GD_OFF = 0

IN_PROJ_ROWS = 256
MERGE_ROWS = 512
MERGE_PARTS = 2
DSA_Q_COLS = 256
DSA_KEY_CHUNK = 512
DSA_VEXT_ROWS = A_KV_DIM + BF16_SUBLANES
GLA_CHUNK = 128
GLA_SUB = 16
GLA_BAND = 4

NT_DIMS = (((1,), (1,)), ((), ()))


def _sigmoid(x):
    return 1.0 / (1.0 + jnp.exp(-x))


def _rms(x, g):
    ms = jnp.mean(x * x, axis=-1, keepdims=True)
    return x * lax.rsqrt(ms + EPS) * g


def _params(*sem):
    return pltpu.CompilerParams(dimension_semantics=sem, vmem_limit_bytes=VMEM_LIMIT_BYTES)


def _in_proj_kernel(x_ref, g_ref, *refs, transposed):
    n_out = len(transposed)
    w_refs, o_refs = refs[:n_out], refs[n_out:]
    h = _rms(x_ref[...], g_ref[...]).astype(BF16)
    for w_ref, o_ref, tr in zip(w_refs, o_refs, transposed):
        if tr:
            z = lax.dot_general(w_ref[...], h, NT_DIMS, preferred_element_type=F32)
        else:
            z = jnp.dot(h, w_ref[...], preferred_element_type=F32)
        o_ref[...] = z.astype(o_ref.dtype)


def _in_proj(x2d, g_pre, w_groups, out_dtypes, transposed):
    T = x2d.shape[0]
    tm = IN_PROJ_ROWS
    in_specs = [pl.BlockSpec((tm, D_MODEL), lambda i: (i, 0)),
                pl.BlockSpec((1, D_MODEL), lambda i: (0, 0))]
    in_specs += [pl.BlockSpec(w.shape, lambda i: (0, 0)) for w in w_groups]
    out_specs, out_shape = [], []
    for w, dt, tr in zip(w_groups, out_dtypes, transposed):
        if tr:
            out_specs.append(pl.BlockSpec((w.shape[0], tm), lambda i: (0, i)))
            out_shape.append(jax.ShapeDtypeStruct((w.shape[0], T), dt))
        else:
            out_specs.append(pl.BlockSpec((tm, w.shape[1]), lambda i: (i, 0)))
            out_shape.append(jax.ShapeDtypeStruct((T, w.shape[1]), dt))
    return pl.pallas_call(
        functools.partial(_in_proj_kernel, transposed=tuple(transposed)),
        grid=(T // tm,),
        in_specs=in_specs,
        out_specs=out_specs,
        out_shape=out_shape,
        compiler_params=_params("arbitrary"),
        name="in_proj",
    )(x2d, g_pre.reshape(1, D_MODEL), *w_groups)


LOG2E = 1.4426950408889634
MAX_ROWS = 4 * 8
F32_SUBLANES = 8
KEY_BITS = 32
BLOCK_KEYS = KEY_BITS * F32_SUBLANES
SELECT_VARIANTS = 4
NEG_INF_UKEY = 0x007FFFFF
MIN_HEALTHY_DENOM = 2.0 ** -100


def _bit_transpose(words):
    a = list(words)
    j, m = 16, 0x0000FFFF
    while j:
        mask = jnp.int32(m if m < 2 ** 31 else m - 2 ** 32)
        k = 0
        while k < KEY_BITS:
            t = (a[k] ^ lax.shift_right_logical(a[k + j], jnp.int32(j))) & mask
            a[k] = a[k] ^ t
            a[k + j] = a[k + j] ^ lax.shift_left(t, jnp.int32(j))
            k = (k + j + 1) & ~j
        j >>= 1
        m = (m ^ (m << j)) & 0xFFFFFFFF
    return a


def _dsa_kernel(qit_ref, wit_ref, qat_ref, ga_ref, ki_ref, ka_ref, vat_ref, o_ref,
                qis_ref, qas_ref, kaug_ref, knorm_ref, vext_ref, sc_ref, planes_ref, cand_ref, sel_ref, ties_ref,
                m_ref, alpha_ref, acc_ref, p_ref, *, tq, ck, top_k):
    qb = pl.program_id(1)
    nh = A_HEADS
    idx_scale = (IDX_HEADS ** -0.5) * (IDX_DIM ** -0.5)
    attn_scale = (A_HEAD_DIM ** -0.5) * LOG2E
    n_chunks = vext_ref.shape[0]
    n_blocks = cand_ref.shape[0]
    blocks_per_chunk = ck // BLOCK_KEYS

    @pl.when(qb == 0)
    def _():
        for j in range(n_chunks):
            vext_ref[j, 0:A_KV_DIM, :] = vat_ref[:, j * ck:(j + 1) * ck]
            vext_ref[j, A_KV_DIM:, :] = jnp.ones((DSA_VEXT_ROWS - A_KV_DIM, ck), BF16)
        planes_ref[...] = jnp.zeros(planes_ref.shape, jnp.int32)
        kaug_ref[:, 0:A_KV_DIM] = ka_ref[...]
        one_col = lax.broadcasted_iota(jnp.int32, (kaug_ref.shape[0], LANES - A_KV_DIM), 1) == 0
        kaug_ref[:, A_KV_DIM:] = jnp.where(one_col, 1.0, 0.0).astype(BF16)
        qas_ref[A_HEAD_DIM + BF16_SUBLANES:, :] = jnp.zeros((LANES - A_HEAD_DIM - BF16_SUBLANES, nh * tq), BF16)
        k32 = ka_ref[...].astype(F32)
        knorm_ref[...] = jnp.sqrt(jnp.max(jnp.sum(k32 * k32, axis=1, keepdims=True), axis=0, keepdims=True))

    qa = (qat_ref[...].astype(F32) * attn_scale).astype(BF16)
    q32 = qa.astype(F32).reshape(nh, A_HEAD_DIM, tq)
    bound = jnp.sqrt(jnp.sum(q32 * q32, axis=1)) * knorm_ref[...]
    first = lax.broadcasted_iota(jnp.int32, (BF16_SUBLANES, tq), 0) == 0
    for h in range(nh):
        cols = slice(h * tq, (h + 1) * tq)
        qis_ref[:, cols] = qit_ref[h * IDX_DIM:(h + 1) * IDX_DIM, :]
        qas_ref[0:A_HEAD_DIM, cols] = qa[h * A_HEAD_DIM:(h + 1) * A_HEAD_DIM, :]
        qas_ref[A_HEAD_DIM:A_HEAD_DIM + BF16_SUBLANES, cols] = jnp.where(first, -bound[h:h + 1, :], 0.0).astype(BF16)

    q0 = qb * tq
    kd = q0 // ck
    nkc = kd + 1
    w = wit_ref[...] * idx_scale

    def store_planes(kc, sc):
        bits = lax.bitcast_convert_type(sc, jnp.int32)
        ukey = bits ^ ((bits >> 31) | jnp.int32(-2 ** 31))
        for j in range(sc.shape[0] // BLOCK_KEYS):
            rows = [ukey[(j * KEY_BITS + e) * F32_SUBLANES:(j * KEY_BITS + e + 1) * F32_SUBLANES, :]
                    for e in range(KEY_BITS)]
            planes = _bit_transpose(rows)
            for b in range(KEY_BITS):
                planes_ref[b, kc * blocks_per_chunk + j] = planes[b]

    def chunk_logits(kc):
        kt = ki_ref[pl.ds(pl.multiple_of(kc * ck, ck), ck), :]
        return jnp.dot(kt, qis_ref[...], preferred_element_type=F32)

    def chunk_scores(lg):
        sc = jnp.zeros((lg.shape[0], tq), F32)
        for h in range(nh):
            sc = sc + w[h:h + 1, :] * jnp.maximum(lg[:, h * tq:(h + 1) * tq], 0.0)
        return sc

    def score_body(kc, carry):
        sc_kc = chunk_scores(chunk_logits(kc))
        sc_ref[kc] = sc_kc
        store_planes(kc, sc_kc)
        return carry

    lax.fori_loop(0, kd, score_body, 0)

    lead = q0 - kd * ck + tq

    def diagonal_scores(rows):
        kt = ki_ref[pl.ds(pl.multiple_of(kd * ck, ck), rows), :]
        sc = chunk_scores(jnp.dot(kt, qis_ref[...], preferred_element_type=F32))
        kpos = kd * ck + lax.broadcasted_iota(jnp.int32, (rows, tq), 0)
        qpos = q0 + lax.broadcasted_iota(jnp.int32, (rows, tq), 1)
        sc = jnp.where(kpos <= qpos, sc, -jnp.inf)
        sc_ref[kd, 0:rows, :] = sc
        store_planes(kd, sc)
        if rows < ck:
            sc_ref[kd, rows:ck, :] = jnp.full((ck - rows, tq), -jnp.inf, F32)
            for j in range(rows // BLOCK_KEYS, blocks_per_chunk):
                for b in range(KEY_BITS):
                    fill = -((NEG_INF_UKEY >> (KEY_BITS - 1 - b)) & 1)
                    planes_ref[b, kd * blocks_per_chunk + j] = jnp.full((F32_SUBLANES, tq), fill, jnp.int32)

    for rows in range(tq, ck + 1, tq):
        pl.when(lead == rows)(functools.partial(diagonal_scores, rows))

    live_blocks = nkc * blocks_per_chunk
    for blk in range(n_blocks):
        cand_ref[blk] = jnp.full((F32_SUBLANES, tq), jnp.where(blk < live_blocks, -1, 0), jnp.int32)

    def select_body(i, carry, blocks):
        need, ukey_thr = carry
        accs = [jnp.zeros((F32_SUBLANES, tq), jnp.int32) for _ in range(4)]
        for blk in range(blocks):
            accs[blk % 4] = accs[blk % 4] + lax.population_count(cand_ref[blk] & planes_ref[i, blk])
        acc = (accs[0] + accs[1]) + (accs[2] + accs[3])
        ones_cnt = jnp.sum(acc, axis=0, keepdims=True)
        take = ones_cnt >= need
        flip = jnp.where(take, 0, -1)
        for blk in range(blocks):
            cand_ref[blk] = cand_ref[blk] & (planes_ref[i, blk] ^ flip)
        need = jnp.where(take, need, need - ones_cnt)
        ukey_thr = ukey_thr | jnp.where(take, lax.shift_left(jnp.int32(1), KEY_BITS - 1 - i), 0)
        return need, ukey_thr

    init = (jnp.full((1, tq), top_k, jnp.int32), jnp.zeros((1, tq), jnp.int32))
    covered = 0
    for blocks in range(n_blocks // SELECT_VARIANTS, n_blocks + 1, n_blocks // SELECT_VARIANTS):
        @pl.when((live_blocks > covered) & (live_blocks <= blocks))
        def _(blocks=blocks):
            found = lax.fori_loop(0, KEY_BITS, functools.partial(select_body, blocks=blocks), init)
            sel_ref[0:1, :] = found[0]
            sel_ref[1:2, :] = found[1]
        covered = blocks
    need = sel_ref[0:1, :]
    ukey_thr = sel_ref[1:2, :]
    thr_bits = jnp.where(ukey_thr < 0, ukey_thr ^ jnp.int32(-2 ** 31), ~ukey_thr)
    thr = lax.bitcast_convert_type(thr_bits, F32)
    qrow = q0 + lax.broadcasted_iota(jnp.int32, (1, tq), 1)
    thr = jnp.where(qrow < top_k, jnp.finfo(F32).min, thr)

    tied = jnp.zeros((F32_SUBLANES, tq), jnp.int32)
    for blk in range(n_blocks):
        tied = tied + lax.population_count(cand_ref[blk])
    surplus = (jnp.sum(tied, axis=0, keepdims=True) > need) & (qrow >= top_k)

    @pl.when(jnp.max(jnp.where(surplus, 1, 0)) > 0)
    def _():
        surplus_bits = jnp.where(surplus, -1, 0)
        for c in range(n_chunks):
            cnt = jnp.zeros((F32_SUBLANES, tq), jnp.int32)
            for j in range(blocks_per_chunk):
                cnt = cnt + lax.population_count(cand_ref[c * blocks_per_chunk + j] & surplus_bits)
            ties_ref[c:c + 1, :] = jnp.sum(cnt, axis=0, keepdims=True)
        any_tie = jnp.max(ties_ref[...], axis=1, keepdims=True) > 0
        chunk_bit = lax.shift_left(1, lax.broadcasted_iota(jnp.int32, (n_chunks, 1), 0))
        chunk_mask = jnp.sum(jnp.where(any_tie, chunk_bit, 0))

        key_row = lax.broadcasted_iota(jnp.int32, (ck, ck), 0)
        key_col = lax.broadcasted_iota(jnp.int32, (ck, ck), 1)
        upto = jnp.where(key_col <= key_row, 1.0, 0.0).astype(BF16)
        need_f = need.astype(F32)

        def drop_body(kc, seen):
            @pl.when((lax.shift_right_logical(chunk_mask, kc) & 1) == 1)
            def _():
                sc = sc_ref[kc]
                tie = jnp.where((sc == thr) & surplus, 1.0, 0.0)
                rank = jnp.dot(upto, tie.astype(BF16), preferred_element_type=F32) + seen
                sc_ref[kc] = jnp.where((tie > 0.0) & (rank > need_f), -jnp.inf, sc)

            return seen + ties_ref[pl.ds(kc, 1), :].astype(F32)

        lax.fori_loop(0, nkc, drop_body, jnp.zeros((1, tq), F32))

    acc_ref[...] = jnp.zeros(acc_ref.shape, F32)

    def fast_chunk(kc, rows):
        off = pl.multiple_of(kc * ck, ck)
        s = jnp.dot(kaug_ref[pl.ds(off, rows), :], qas_ref[...], preferred_element_type=F32)
        neg = jnp.where(sc_ref[kc, 0:rows, :] >= thr, 0.0, -jnp.inf)
        for h in range(nh):
            cols = slice(h * tq, (h + 1) * tq)
            p_ref[0:rows, cols] = jnp.exp2(s[:, cols] + neg).astype(BF16)
        pv = jnp.dot(vext_ref[kc, :, 0:rows], p_ref[0:rows, :], preferred_element_type=F32)
        for h in range(nh):
            acc_ref[h] = acc_ref[h] + pv[:, h * tq:(h + 1) * tq]

    def fast_body(kc, carry):
        fast_chunk(kc, ck)
        return carry

    lax.fori_loop(0, kd, fast_body, 0)
    for rows in range(tq, ck + 1, tq):
        pl.when(lead == rows)(functools.partial(fast_chunk, kd, rows))

    denom = acc_ref[0, A_KV_DIM:A_KV_DIM + F32_SUBLANES, :]
    for h in range(1, nh):
        denom = jnp.minimum(denom, acc_ref[h, A_KV_DIM:A_KV_DIM + F32_SUBLANES, :])
    healthy = jnp.min(denom) >= MIN_HEALTHY_DENOM

    @pl.when(jnp.logical_not(healthy))
    def _():
        m_ref[...] = jnp.full(m_ref.shape, jnp.finfo(F32).min, F32)
        acc_ref[...] = jnp.zeros(acc_ref.shape, F32)

        def attn_body(kc, carry):
            off = pl.multiple_of(kc * ck, ck)
            s = jnp.dot(ka_ref[pl.ds(off, ck), :], qas_ref[0:A_HEAD_DIM, :], preferred_element_type=F32)
            neg = jnp.where(sc_ref[kc] >= thr, 0.0, -jnp.inf)
            for h in range(nh):
                cols = slice(h * tq, (h + 1) * tq)
                sh = s[:, cols] + neg
                cm = jnp.max(sh.reshape(ck // MAX_ROWS, MAX_ROWS, tq), axis=0)
                m_old = m_ref[h:h + 1, :]
                m_new = jnp.maximum(m_old, jnp.max(cm, axis=0, keepdims=True))
                alpha_ref[h:h + 1, :] = jnp.exp2(m_old - m_new)
                m_ref[h:h + 1, :] = m_new
                p_ref[:, cols] = jnp.exp2(sh - m_new).astype(BF16)
            pv = jnp.dot(vext_ref[kc], p_ref[...], preferred_element_type=F32)
            for h in range(nh):
                acc_ref[h] = alpha_ref[h:h + 1, :] * acc_ref[h] + pv[:, h * tq:(h + 1) * tq]
            return carry

        lax.fori_loop(0, nkc, attn_body, 0)

    o_t = []
    for h in range(nh):
        a = acc_ref[h]
        o_t.append(a[0:A_KV_DIM, :] / a[A_KV_DIM:A_KV_DIM + 1, :])
    o = jnp.concatenate(o_t, axis=0).T
    g = ga_ref[...]
    o_ref[...] = (o * (g * _sigmoid(g))).astype(o_ref.dtype)


def _dsa(qit, wit, qat, ga, ki, ka, vat, B, L):
    tq, ck = DSA_Q_COLS, min(DSA_KEY_CHUNK, L)
    top_k = min(TOPK_MAX, L // 4)
    nq = L // tq
    nck = L // ck
    assert L % ck == 0 and ck % BLOCK_KEYS == 0 and (L // BLOCK_KEYS) % SELECT_VARIANTS == 0
    col = lambda b, q: (0, b * nq + q)
    kernel = functools.partial(_dsa_kernel, tq=tq, ck=ck, top_k=top_k)
    return pl.pallas_call(
        kernel,
        grid=(B, nq),
        in_specs=[pl.BlockSpec((A_WIDTH, tq), col),
                  pl.BlockSpec((IDX_HEADS, tq), col),
                  pl.BlockSpec((A_WIDTH, tq), col),
                  pl.BlockSpec((tq, A_WIDTH), lambda b, q: (b * nq + q, 0)),
                  pl.BlockSpec((L, IDX_DIM), lambda b, q: (b, 0)),
                  pl.BlockSpec((L, A_KV_DIM), lambda b, q: (b, 0)),
                  pl.BlockSpec((A_KV_DIM, L), lambda b, q: (0, b))],
        out_specs=pl.BlockSpec((tq, A_WIDTH), lambda b, q: (b * nq + q, 0)),
        out_shape=jax.ShapeDtypeStruct((B * L, A_WIDTH), BF16),
        scratch_shapes=[pltpu.VMEM((IDX_DIM, A_HEADS * tq), BF16),
                        pltpu.VMEM((LANES, A_HEADS * tq), BF16),
                        pltpu.VMEM((L, LANES), BF16),
                        pltpu.VMEM((1, 1), F32),
                        pltpu.VMEM((nck, DSA_VEXT_ROWS, ck), BF16),
                        pltpu.VMEM((nck, ck, tq), F32),
                        pltpu.VMEM((KEY_BITS, L // BLOCK_KEYS, F32_SUBLANES, tq), jnp.int32),
                        pltpu.VMEM((L // BLOCK_KEYS, F32_SUBLANES, tq), jnp.int32),
                        pltpu.VMEM((F32_SUBLANES, tq), jnp.int32),
                        pltpu.VMEM((nck, tq), jnp.int32),
                        pltpu.VMEM((A_HEADS, tq), F32),
                        pltpu.VMEM((A_HEADS, tq), F32),
                        pltpu.VMEM((A_HEADS, DSA_VEXT_ROWS, tq), F32),
                        pltpu.VMEM((ck, A_HEADS * tq), BF16)],
        compiler_params=_params("arbitrary", "arbitrary"),
        name="dsa",
    )(qit, wit, qat, ga, ki, ka, vat)


def _gla_kernel(q_ref, k_ref, v_ref, gb_ref, small_ref, wup_ref, bg_ref, gh_ref, o_ref, s_ref,
                *, c, sub, band):
    @pl.when(pl.program_id(1) == 0)
    def _():
        s_ref[...] = jnp.zeros(s_ref.shape, F32)

    hi = lax.Precision.HIGHEST
    row = lax.broadcasted_iota(jnp.int32, (c, c), 0)
    col = lax.broadcasted_iota(jnp.int32, (c, c), 1)
    tri = jnp.where(col <= row, 1.0, 0.0)
    lane = lax.broadcasted_iota(jnp.int32, (B_KEY_DIM, c), 1)
    krow = lax.broadcasted_iota(jnp.int32, (c, B_KEY_DIM), 0)
    diff = col - row
    ones = jnp.ones((B_KEY_DIM, c), BF16)
    small = small_ref[...]
    halves = []
    half = sub // 2
    while half >= band:
        halves.append((half, (row // (2 * half)) == (col // (2 * half))))
        half //= 2

    heads = range(B_HEADS)
    kcols = [slice(h * B_KEY_DIM, (h + 1) * B_KEY_DIM) for h in heads]
    vcols = [slice(h * B_VAL_DIM, (h + 1) * B_VAL_DIM) for h in heads]
    q = [q_ref[:, kcols[h]] * (B_KEY_DIM ** -0.5) for h in heads]
    k = [k_ref[:, kcols[h]] for h in heads]
    v = [v_ref[:, vcols[h]] for h in heads]
    xg = [jnp.dot(small, wup_ref[:, kcols[h]], precision=hi, preferred_element_type=F32) + bg_ref[:, kcols[h]]
          for h in heads]
    la = [(jnp.minimum(x, 0.0) - jnp.log(1.0 + jnp.exp(-jnp.abs(x)))) * (LOG2E / GATE_TAU) for x in xg]
    g = [jnp.dot(tri, l, precision=hi, preferred_element_type=F32) for l in la]
    gt = [x.T for x in g]
    kt = [x.T for x in k]

    o = [jnp.dot((q[h] * jnp.exp2(g[h])).astype(BF16), s_ref[h].astype(BF16), preferred_element_type=F32)
         for h in heads]

    a_rows = [[jnp.zeros((sub, c), F32)] for h in heads]
    for i in range(1, c // sub):
        lo = i * sub
        for h in heads:
            expo = jnp.where(lane < lo, gt[h][:, lo - 1:lo] - gt[h], -jnp.inf)
            kti = (kt[h] * jnp.exp2(expo)).astype(BF16)
            qi = (q[h][lo:lo + sub, :] * jnp.exp2(g[h][lo:lo + sub, :] - g[h][lo - 1:lo, :])).astype(BF16)
            a_rows[h].append(jnp.dot(qi, kti, preferred_element_type=F32))
    a = [jnp.concatenate(a_rows[h], axis=0) for h in heads]

    for half, same_block in halves:
        size = 2 * half
        pos = krow % size
        for h in heads:
            g3 = g[h].reshape(c // size, size, B_KEY_DIM)
            ref = jnp.broadcast_to(g3[:, half - 1:half, :], g3.shape).reshape(c, B_KEY_DIM)
            ql = (q[h] * jnp.exp2(jnp.where(pos >= half, g[h] - ref, -jnp.inf))).astype(BF16)
            kl = (k[h] * jnp.exp2(jnp.where(pos < half, ref - g[h], -jnp.inf))).astype(BF16)
            al = lax.dot_general(ql, kl, NT_DIMS, preferred_element_type=F32)
            a[h] = a[h] + jnp.where(same_block, al, 0.0)

    rmod = krow % band
    for d in range(band):
        for h in heads:
            ks = k[h] if d == 0 else pltpu.roll(k[h], d, axis=0)
            gs = g[h] if d == 0 else pltpu.roll(g[h], d, axis=0)
            e = jnp.where(rmod >= d, g[h] - gs, -jnp.inf)
            r = jnp.dot((q[h] * ks * jnp.exp2(e)).astype(BF16), ones, preferred_element_type=F32)
            a[h] = a[h] + jnp.where(diff == -d, r, 0.0)

    for h in heads:
        oh = o[h] + jnp.dot(a[h].astype(BF16), v[h], preferred_element_type=F32)
        glast = gt[h][:, c - 1:c]
        kend = (kt[h] * jnp.exp2(glast - gt[h])).astype(BF16)
        s_ref[h] = jnp.exp2(glast) * s_ref[h] + jnp.dot(kend, v[h], preferred_element_type=F32)
        gb = gb_ref[:, vcols[h]]
        o_ref[:, vcols[h]] = (_rms(oh, gh_ref[...]) * (gb * _sigmoid(gb))).astype(o_ref.dtype)


def _gla(qb, kb, vb, gb, small, wup_pad, b_gate, g_head, B, L):
    c = GLA_CHUNK
    nc = L // c
    row = lambda b, i: (b * nc + i, 0)
    fixed = lambda b, i: (0, 0)
    return pl.pallas_call(
        functools.partial(_gla_kernel, c=c, sub=GLA_SUB, band=GLA_BAND),
        grid=(B, nc),
        in_specs=[pl.BlockSpec((c, B_KEY_WIDTH), row),
                  pl.BlockSpec((c, B_KEY_WIDTH), row),
                  pl.BlockSpec((c, B_VAL_WIDTH), row),
                  pl.BlockSpec((c, B_VAL_WIDTH), row),
                  pl.BlockSpec((c, LANES), row),
                  pl.BlockSpec((LANES, B_KEY_WIDTH), fixed),
                  pl.BlockSpec((1, B_KEY_WIDTH), fixed),
                  pl.BlockSpec((1, B_VAL_DIM), fixed)],
        out_specs=pl.BlockSpec((c, B_VAL_WIDTH), row),
        out_shape=jax.ShapeDtypeStruct((B * L, B_VAL_WIDTH), BF16),
        scratch_shapes=[pltpu.VMEM((B_HEADS, B_KEY_DIM, B_VAL_DIM), F32)],
        compiler_params=_params("arbitrary", "arbitrary"),
        name="gla",
    )(qb, kb, vb, gb, small, wup_pad, b_gate.reshape(1, B_KEY_WIDTH), g_head.reshape(1, B_VAL_DIM))


def _merge_kernel(x_ref, p_ref, oa_ref, ob_ref, ma_ref, mb_ref, wpa_ref, wpb_ref, wout_ref,
                  gpost_ref, wple_ref, wpg_ref, gpre_ref, gpost2_ref, o_ref, *, parts):
    rows = x_ref.shape[0] // parts
    sl = [slice(i * rows, (i + 1) * rows) for i in range(parts)]
    dot = lambda a, w_ref: jnp.dot(a, w_ref[...], preferred_element_type=F32)
    ya = [dot(oa_ref[r, :], wpa_ref) for r in sl]
    yb = [dot(ob_ref[r, :], wpb_ref) for r in sl]
    y = [_sigmoid(ma_ref[r, :]) * a + _sigmoid(mb_ref[r, :]) * b for r, a, b in zip(sl, ya, yb)]
    u = [dot(t.astype(BF16), wout_ref) for t in y]
    x1 = [x_ref[r, :] + _rms(t, gpost_ref[...]) for r, t in zip(sl, u)]
    gate = [dot(_rms(t, gpre_ref[...]).astype(BF16), wpg_ref) for t in x1]
    e = [dot(p_ref[r, :].astype(BF16), wple_ref) * _sigmoid(t) for r, t in zip(sl, gate)]
    for r, a, b in zip(sl, x1, e):
        o_ref[r, :] = a + _rms(b, gpost2_ref[...])


def _merge(x2d, p2d, oa, ob, ma, mb, wpa, wpb, wout, g_post, wple, wpg, g_pre2, g_post2):
    T = x2d.shape[0]
    tm = MERGE_ROWS
    rows = lambda width: pl.BlockSpec((tm, width), lambda i: (i, 0))
    full = lambda a: pl.BlockSpec(a.shape, lambda i: (0, 0))
    vec = lambda a: a.reshape(1, D_MODEL)
    args = (x2d, p2d, oa, ob, ma, mb, wpa, wpb, wout, vec(g_post), wple, wpg, vec(g_pre2), vec(g_post2))
    in_specs = [rows(D_MODEL), rows(PLE_DIM), rows(A_WIDTH), rows(B_VAL_WIDTH), rows(D_MODEL), rows(D_MODEL)]
    in_specs += [full(a) for a in args[6:]]
    return pl.pallas_call(
        functools.partial(_merge_kernel, parts=MERGE_PARTS),
        grid=(T // tm,),
        in_specs=in_specs,
        out_specs=rows(D_MODEL),
        out_shape=jax.ShapeDtypeStruct((T, D_MODEL), F32),
        compiler_params=_params("arbitrary"),
        name="merge",
    )(*args)


def _split_w_in(w_in):
    offs = [0]
    for s in IN_SPLITS:
        offs.append(offs[-1] + s)
    col = lambda i: w_in[:, offs[i]:offs[i + 1]]
    qa, ka, va, qi, ki, wi, ga, qb, kb, vb, gdown, gb, ma, mb = [col(i) for i in range(len(IN_SPLITS))]
    pad = jnp.zeros((D_MODEL, LANES - GATE_RANK), w_in.dtype)
    small = jnp.concatenate([gdown, pad], axis=1)
    groups = [qa.T, ka, va.T, qi.T, ki, wi.T, small, ga, qb, kb, vb, gb, ma, mb]
    dtypes = [BF16, BF16, BF16, BF16, BF16, F32, F32, F32, F32, F32, BF16, F32, F32, F32]
    transposed = [True, False, True, True, False, True] + [False] * 8
    return [g.astype(BF16) for g in groups], dtypes, transposed


def _layer(x2d, p2d, B, L, g_pre, w_in, w_gate_up, b_gate, g_gla_head, w_proj_a, w_proj_b,
           w_out, g_post, w_ple, w_ple_gate, g_ple_pre, g_ple_post):
    groups, dtypes, transposed = _split_w_in(w_in)
    qat, ka, vat, qit, ki, wit, small, ga, qb, kb, vb, gb, ma, mb = _in_proj(
        x2d, g_pre, groups, dtypes, transposed)
    oa = _dsa(qit, wit, qat, ga, ki, ka, vat, B, L)
    wup_pad = jnp.zeros((LANES, B_KEY_WIDTH), F32).at[SMALL_GD_OFF:SMALL_GD_OFF + GATE_RANK].set(w_gate_up)
    ob = _gla(qb, kb, vb, gb, small, wup_pad, b_gate, g_gla_head, B, L)
    bf = lambda a: a.astype(BF16)
    return _merge(x2d, p2d, oa, ob, ma, mb, bf(w_proj_a), bf(w_proj_b), bf(w_out), g_post,
                  bf(w_ple), bf(w_ple_gate), g_ple_pre, g_ple_post)


def kernel(x, p, g_pre, w_in, w_gate_up, b_gate, g_gla_head, w_proj_a, w_proj_b, w_out, g_post,
           w_ple, w_ple_gate, g_ple_pre, g_ple_post):
    B, L, _ = x.shape
    depth = p.shape[0]
    x2d = x.reshape(B * L, D_MODEL)
    for i in range(depth):
        x2d = _layer(x2d, p[i].reshape(B * L, PLE_DIM), B, L, g_pre[i], w_in[i], w_gate_up[i],
                     b_gate[i], g_gla_head[i], w_proj_a[i], w_proj_b[i], w_out[i], g_post[i],
                     w_ple[i], w_ple_gate[i], g_ple_pre[i], g_ple_post[i])
    return x2d.reshape(B, L, D_MODEL)
```

```python
import functools

import jax
import jax.numpy as jnp
from jax import lax
from jax.experimental import pallas as pl
from jax.experimental.pallas import tpu as pltpu

F32 = jnp.float32
BF16 = jnp.bfloat16

D_MODEL = 1024
PLE_DIM = 256
A_HEADS = 8
A_HEAD_DIM = 64
A_WIDTH = A_HEADS * A_HEAD_DIM
A_KV_DIM = 64
IDX_HEADS = 8
IDX_DIM = 64
TOPK_MAX = 256
B_HEADS = 4
B_KEY_DIM = 128
B_VAL_DIM = 256
B_KEY_WIDTH = B_HEADS * B_KEY_DIM
B_VAL_WIDTH = B_HEADS * B_VAL_DIM
GATE_RANK = 16
GATE_TAU = 16.0
EPS = 1e-6

IN_SPLITS = (A_WIDTH, A_KV_DIM, A_KV_DIM, IDX_HEADS * IDX_DIM, IDX_DIM, IDX_HEADS, A_WIDTH,
             B_KEY_WIDTH, B_KEY_WIDTH, B_VAL_WIDTH, GATE_RANK, B_VAL_WIDTH,
             D_MODEL, D_MODEL)

VMEM_LIMIT_BYTES = 56 * 1024 * 1024
LANES = 128
BF16_SUBLANES = 16

SMALL_GD_OFF = 0

IN_PROJ_ROWS = 512
IN_PROJ_PARTS = 2
MERGE_ROWS = 512
MERGE_PARTS = 2
DSA_Q_COLS = 256
DSA_KEY_CHUNK = 512
DSA_VEXT_ROWS = A_KV_DIM + BF16_SUBLANES
GLA_CHUNK = 128
GLA_SUB = 16
GLA_BAND = 4

NT_DIMS = (((1,), (1,)), ((), ()))


def _sigmoid(x):
    return 1.0 / (1.0 + jnp.exp(-x))


def _rms(x, g):
    ms = jnp.mean(x * x, axis=-1, keepdims=True)
    return x * lax.rsqrt(ms + EPS) * g


def _params(*sem):
    return pltpu.CompilerParams(dimension_semantics=sem, vmem_limit_bytes=VMEM_LIMIT_BYTES)


def _in_proj_kernel(x_ref, g_ref, *refs, transposed, parts):
    n_out = len(transposed)
    w_refs, o_refs = refs[:n_out], refs[n_out:]
    rows = x_ref.shape[0] // parts
    sl = [slice(i * rows, (i + 1) * rows) for i in range(parts)]
    h = [_rms(x_ref[r, :], g_ref[...]).astype(BF16) for r in sl]
    for w_ref, o_ref, tr in zip(w_refs, o_refs, transposed):
        for r, hp in zip(sl, h):
            if tr:
                o_ref[:, r] = lax.dot_general(w_ref[...], hp, NT_DIMS, preferred_element_type=F32).astype(o_ref.dtype)
            else:
                o_ref[r, :] = jnp.dot(hp, w_ref[...], preferred_element_type=F32).astype(o_ref.dtype)


def _in_proj(x2d, g_pre, w_groups, out_dtypes, transposed):
    T = x2d.shape[0]
    tm = IN_PROJ_ROWS
    in_specs = [pl.BlockSpec((tm, D_MODEL), lambda i: (i, 0)),
                pl.BlockSpec((1, D_MODEL), lambda i: (0, 0))]
    in_specs += [pl.BlockSpec(w.shape, lambda i: (0, 0), pipeline_mode=pl.Buffered(1)) for w in w_groups]
    out_specs, out_shape = [], []
    for w, dt, tr in zip(w_groups, out_dtypes, transposed):
        if tr:
            out_specs.append(pl.BlockSpec((w.shape[0], tm), lambda i: (0, i)))
            out_shape.append(jax.ShapeDtypeStruct((w.shape[0], T), dt))
        else:
            out_specs.append(pl.BlockSpec((tm, w.shape[1]), lambda i: (i, 0)))
            out_shape.append(jax.ShapeDtypeStruct((T, w.shape[1]), dt))
    return pl.pallas_call(
        functools.partial(_in_proj_kernel, transposed=tuple(transposed), parts=IN_PROJ_PARTS),
        grid=(T // tm,),
        in_specs=in_specs,
        out_specs=out_specs,
        out_shape=out_shape,
        compiler_params=_params("arbitrary"),
        name="in_proj",
    )(x2d, g_pre.reshape(1, D_MODEL), *w_groups)


LOG2E = 1.4426950408889634
MAX_ROWS = 4 * 8
F32_SUBLANES = 8
KEY_BITS = 32
BLOCK_KEYS = KEY_BITS * F32_SUBLANES
SELECT_VARIANTS = 4
NEG_INF_UKEY = 0x007FFFFF
MIN_HEALTHY_DENOM = 2.0 ** -100


def _bit_transpose(words):
    a = list(words)
    j, m = 16, 0x0000FFFF
    while j:
        mask = jnp.int32(m if m < 2 ** 31 else m - 2 ** 32)
        k = 0
        while k < KEY_BITS:
            t = (a[k] ^ lax.shift_right_logical(a[k + j], jnp.int32(j))) & mask
            a[k] = a[k] ^ t
            a[k + j] = a[k + j] ^ lax.shift_left(t, jnp.int32(j))
            k = (k + j + 1) & ~j
        j >>= 1
        m = (m ^ (m << j)) & 0xFFFFFFFF
    return a


def _dsa_kernel(qit_ref, wit_ref, qat_ref, ga_ref, ki_ref, ka_ref, vat_ref, o_ref,
                qis_ref, qas_ref, kaug_ref, knorm_ref, vext_ref, sc_ref, planes_ref, cand_ref, sel_ref, ties_ref,
                m_ref, alpha_ref, acc_ref, p_ref, *, tq, ck, top_k):
    qb = pl.program_id(1)
    nh = A_HEADS
    idx_scale = (IDX_HEADS ** -0.5) * (IDX_DIM ** -0.5)
    attn_scale = (A_HEAD_DIM ** -0.5) * LOG2E
    n_chunks = vext_ref.shape[0]
    n_blocks = cand_ref.shape[0]
    blocks_per_chunk = ck // BLOCK_KEYS

    @pl.when(qb == 0)
    def _():
        for j in range(n_chunks):
            vext_ref[j, 0:A_KV_DIM, :] = vat_ref[:, j * ck:(j + 1) * ck]
            vext_ref[j, A_KV_DIM:, :] = jnp.ones((DSA_VEXT_ROWS - A_KV_DIM, ck), BF16)
        planes_ref[...] = jnp.zeros(planes_ref.shape, jnp.int32)
        kaug_ref[:, 0:A_KV_DIM] = ka_ref[...]
        one_col = lax.broadcasted_iota(jnp.int32, (kaug_ref.shape[0], LANES - A_KV_DIM), 1) == 0
        kaug_ref[:, A_KV_DIM:] = jnp.where(one_col, 1.0, 0.0).astype(BF16)
        qas_ref[A_HEAD_DIM + BF16_SUBLANES:, :] = jnp.zeros((LANES - A_HEAD_DIM - BF16_SUBLANES, nh * tq), BF16)
        k32 = ka_ref[...].astype(F32)
        knorm_ref[...] = jnp.sqrt(jnp.max(jnp.sum(k32 * k32, axis=1, keepdims=True), axis=0, keepdims=True))

    qa = (qat_ref[...].astype(F32) * attn_scale).astype(BF16)
    q32 = qa.astype(F32).reshape(nh, A_HEAD_DIM, tq)
    bound = jnp.sqrt(jnp.sum(q32 * q32, axis=1)) * knorm_ref[...]
    first = lax.broadcasted_iota(jnp.int32, (BF16_SUBLANES, tq), 0) == 0
    for h in range(nh):
        cols = slice(h * tq, (h + 1) * tq)
        qis_ref[:, cols] = qit_ref[h * IDX_DIM:(h + 1) * IDX_DIM, :]
        qas_ref[0:A_HEAD_DIM, cols] = qa[h * A_HEAD_DIM:(h + 1) * A_HEAD_DIM, :]
        qas_ref[A_HEAD_DIM:A_HEAD_DIM + BF16_SUBLANES, cols] = jnp.where(first, -bound[h:h + 1, :], 0.0).astype(BF16)

    q0 = qb * tq
    kd = q0 // ck
    nkc = kd + 1
    w = wit_ref[...] * idx_scale

    def store_planes(kc, sc):
        bits = lax.bitcast_convert_type(sc, jnp.int32)
        ukey = bits ^ ((bits >> 31) | jnp.int32(-2 ** 31))
        for j in range(sc.shape[0] // BLOCK_KEYS):
            rows = [ukey[(j * KEY_BITS + e) * F32_SUBLANES:(j * KEY_BITS + e + 1) * F32_SUBLANES, :]
                    for e in range(KEY_BITS)]
            planes = _bit_transpose(rows)
            for b in range(KEY_BITS):
                planes_ref[b, kc * blocks_per_chunk + j] = planes[b]

    def chunk_logits(kc):
        kt = ki_ref[pl.ds(pl.multiple_of(kc * ck, ck), ck), :]
        return jnp.dot(kt, qis_ref[...], preferred_element_type=F32)

    def chunk_scores(lg):
        sc = jnp.zeros((lg.shape[0], tq), F32)
        for h in range(nh):
            sc = sc + w[h:h + 1, :] * jnp.maximum(lg[:, h * tq:(h + 1) * tq], 0.0)
        return sc

    def score_body(kc, carry):
        sc_kc = chunk_scores(chunk_logits(kc))
        sc_ref[kc] = sc_kc
        store_planes(kc, sc_kc)
        return carry

    lax.fori_loop(0, kd, score_body, 0)

    lead = q0 - kd * ck + tq

    def diagonal_scores(rows):
        kt = ki_ref[pl.ds(pl.multiple_of(kd * ck, ck), rows), :]
        sc = chunk_scores(jnp.dot(kt, qis_ref[...], preferred_element_type=F32))
        kpos = kd * ck + lax.broadcasted_iota(jnp.int32, (rows, tq), 0)
        qpos = q0 + lax.broadcasted_iota(jnp.int32, (rows, tq), 1)
        sc = jnp.where(kpos <= qpos, sc, -jnp.inf)
        sc_ref[kd, 0:rows, :] = sc
        store_planes(kd, sc)
        if rows < ck:
            sc_ref[kd, rows:ck, :] = jnp.full((ck - rows, tq), -jnp.inf, F32)
            for j in range(rows // BLOCK_KEYS, blocks_per_chunk):
                for b in range(KEY_BITS):
                    fill = -((NEG_INF_UKEY >> (KEY_BITS - 1 - b)) & 1)
                    planes_ref[b, kd * blocks_per_chunk + j] = jnp.full((F32_SUBLANES, tq), fill, jnp.int32)

    for rows in range(tq, ck + 1, tq):
        pl.when(lead == rows)(functools.partial(diagonal_scores, rows))

    live_blocks = nkc * blocks_per_chunk
    for blk in range(n_blocks):
        cand_ref[blk] = jnp.full((F32_SUBLANES, tq), jnp.where(blk < live_blocks, -1, 0), jnp.int32)

    def select_body(i, carry, blocks):
        need, ukey_thr = carry
        accs = [jnp.zeros((F32_SUBLANES, tq), jnp.int32) for _ in range(4)]
        for blk in range(blocks):
            accs[blk % 4] = accs[blk % 4] + lax.population_count(cand_ref[blk] & planes_ref[i, blk])
        acc = (accs[0] + accs[1]) + (accs[2] + accs[3])
        ones_cnt = jnp.sum(acc, axis=0, keepdims=True)
        take = ones_cnt >= need
        flip = jnp.where(take, 0, -1)
        for blk in range(blocks):
            cand_ref[blk] = cand_ref[blk] & (planes_ref[i, blk] ^ flip)
        need = jnp.where(take, need, need - ones_cnt)
        ukey_thr = ukey_thr | jnp.where(take, lax.shift_left(jnp.int32(1), KEY_BITS - 1 - i), 0)
        return need, ukey_thr

    init = (jnp.full((1, tq), top_k, jnp.int32), jnp.zeros((1, tq), jnp.int32))
    covered = 0
    for blocks in range(n_blocks // SELECT_VARIANTS, n_blocks + 1, n_blocks // SELECT_VARIANTS):
        @pl.when((live_blocks > covered) & (live_blocks <= blocks))
        def _(blocks=blocks):
            found = lax.fori_loop(0, KEY_BITS, functools.partial(select_body, blocks=blocks), init)
            sel_ref[0:1, :] = found[0]
            sel_ref[1:2, :] = found[1]
        covered = blocks
    need = sel_ref[0:1, :]
    ukey_thr = sel_ref[1:2, :]
    thr_bits = jnp.where(ukey_thr < 0, ukey_thr ^ jnp.int32(-2 ** 31), ~ukey_thr)
    thr = lax.bitcast_convert_type(thr_bits, F32)
    qrow = q0 + lax.broadcasted_iota(jnp.int32, (1, tq), 1)
    thr = jnp.where(qrow < top_k, jnp.finfo(F32).min, thr)

    tied = jnp.zeros((F32_SUBLANES, tq), jnp.int32)
    for blk in range(n_blocks):
        tied = tied + lax.population_count(cand_ref[blk])
    surplus = (jnp.sum(tied, axis=0, keepdims=True) > need) & (qrow >= top_k)

    @pl.when(jnp.max(jnp.where(surplus, 1, 0)) > 0)
    def _():
        surplus_bits = jnp.where(surplus, -1, 0)
        for c in range(n_chunks):
            cnt = jnp.zeros((F32_SUBLANES, tq), jnp.int32)
            for j in range(blocks_per_chunk):
                cnt = cnt + lax.population_count(cand_ref[c * blocks_per_chunk + j] & surplus_bits)
            ties_ref[c:c + 1, :] = jnp.sum(cnt, axis=0, keepdims=True)
        any_tie = jnp.max(ties_ref[...], axis=1, keepdims=True) > 0
        chunk_bit = lax.shift_left(1, lax.broadcasted_iota(jnp.int32, (n_chunks, 1), 0))
        chunk_mask = jnp.sum(jnp.where(any_tie, chunk_bit, 0))

        key_row = lax.broadcasted_iota(jnp.int32, (ck, ck), 0)
        key_col = lax.broadcasted_iota(jnp.int32, (ck, ck), 1)
        upto = jnp.where(key_col <= key_row, 1.0, 0.0).astype(BF16)
        need_f = need.astype(F32)

        def drop_body(kc, seen):
            @pl.when((lax.shift_right_logical(chunk_mask, kc) & 1) == 1)
            def _():
                sc = sc_ref[kc]
                tie = jnp.where((sc == thr) & surplus, 1.0, 0.0)
                rank = jnp.dot(upto, tie.astype(BF16), preferred_element_type=F32) + seen
                sc_ref[kc] = jnp.where((tie > 0.0) & (rank > need_f), -jnp.inf, sc)

            return seen + ties_ref[pl.ds(kc, 1), :].astype(F32)

        lax.fori_loop(0, nkc, drop_body, jnp.zeros((1, tq), F32))

    acc_ref[...] = jnp.zeros(acc_ref.shape, F32)

    def fast_chunk(kc, rows):
        off = pl.multiple_of(kc * ck, ck)
        s = jnp.dot(kaug_ref[pl.ds(off, rows), :], qas_ref[...], preferred_element_type=F32)
        neg = jnp.where(sc_ref[kc, 0:rows, :] >= thr, 0.0, -jnp.inf)
        for h in range(nh):
            cols = slice(h * tq, (h + 1) * tq)
            p_ref[0:rows, cols] = jnp.exp2(s[:, cols] + neg).astype(BF16)
        pv = jnp.dot(vext_ref[kc, :, 0:rows], p_ref[0:rows, :], preferred_element_type=F32)
        for h in range(nh):
            acc_ref[h] = acc_ref[h] + pv[:, h * tq:(h + 1) * tq]

    def fast_body(kc, carry):
        fast_chunk(kc, ck)
        return carry

    lax.fori_loop(0, kd, fast_body, 0)
    for rows in range(tq, ck + 1, tq):
        pl.when(lead == rows)(functools.partial(fast_chunk, kd, rows))

    denom = acc_ref[0, A_KV_DIM:A_KV_DIM + F32_SUBLANES, :]
    for h in range(1, nh):
        denom = jnp.minimum(denom, acc_ref[h, A_KV_DIM:A_KV_DIM + F32_SUBLANES, :])
    healthy = jnp.min(denom) >= MIN_HEALTHY_DENOM

    @pl.when(jnp.logical_not(healthy))
    def _():
        m_ref[...] = jnp.full(m_ref.shape, jnp.finfo(F32).min, F32)
        acc_ref[...] = jnp.zeros(acc_ref.shape, F32)

        def attn_body(kc, carry):
            off = pl.multiple_of(kc * ck, ck)
            s = jnp.dot(ka_ref[pl.ds(off, ck), :], qas_ref[0:A_HEAD_DIM, :], preferred_element_type=F32)
            neg = jnp.where(sc_ref[kc] >= thr, 0.0, -jnp.inf)
            for h in range(nh):
                cols = slice(h * tq, (h + 1) * tq)
                sh = s[:, cols] + neg
                cm = jnp.max(sh.reshape(ck // MAX_ROWS, MAX_ROWS, tq), axis=0)
                m_old = m_ref[h:h + 1, :]
                m_new = jnp.maximum(m_old, jnp.max(cm, axis=0, keepdims=True))
                alpha_ref[h:h + 1, :] = jnp.exp2(m_old - m_new)
                m_ref[h:h + 1, :] = m_new
                p_ref[:, cols] = jnp.exp2(sh - m_new).astype(BF16)
            pv = jnp.dot(vext_ref[kc], p_ref[...], preferred_element_type=F32)
            for h in range(nh):
                acc_ref[h] = alpha_ref[h:h + 1, :] * acc_ref[h] + pv[:, h * tq:(h + 1) * tq]
            return carry

        lax.fori_loop(0, nkc, attn_body, 0)

    o_t = []
    for h in range(nh):
        a = acc_ref[h]
        o_t.append(a[0:A_KV_DIM, :] / a[A_KV_DIM:A_KV_DIM + 1, :])
    o = jnp.concatenate(o_t, axis=0).T
    g = ga_ref[...]
    o_ref[...] = (o * (g * _sigmoid(g))).astype(o_ref.dtype)


def _dsa(qit, wit, qat, ga, ki, ka, vat, B, L):
    tq, ck = DSA_Q_COLS, min(DSA_KEY_CHUNK, L)
    top_k = min(TOPK_MAX, L // 4)
    nq = L // tq
    nck = L // ck
    assert L % ck == 0 and ck % tq == 0 and tq % BLOCK_KEYS == 0 and (L // BLOCK_KEYS) % SELECT_VARIANTS == 0
    col = lambda b, q: (0, b * nq + q)
    kernel = functools.partial(_dsa_kernel, tq=tq, ck=ck, top_k=top_k)
    return pl.pallas_call(
        kernel,
        grid=(B, nq),
        in_specs=[pl.BlockSpec((A_WIDTH, tq), col),
                  pl.BlockSpec((IDX_HEADS, tq), col),
                  pl.BlockSpec((A_WIDTH, tq), col),
                  pl.BlockSpec((tq, A_WIDTH), lambda b, q: (b * nq + q, 0)),
                  pl.BlockSpec((L, IDX_DIM), lambda b, q: (b, 0)),
                  pl.BlockSpec((L, A_KV_DIM), lambda b, q: (b, 0)),
                  pl.BlockSpec((A_KV_DIM, L), lambda b, q: (0, b))],
        out_specs=pl.BlockSpec((tq, A_WIDTH), lambda b, q: (b * nq + q, 0)),
        out_shape=jax.ShapeDtypeStruct((B * L, A_WIDTH), BF16),
        scratch_shapes=[pltpu.VMEM((IDX_DIM, A_HEADS * tq), BF16),
                        pltpu.VMEM((LANES, A_HEADS * tq), BF16),
                        pltpu.VMEM((L, LANES), BF16),
                        pltpu.VMEM((1, 1), F32),
                        pltpu.VMEM((nck, DSA_VEXT_ROWS, ck), BF16),
                        pltpu.VMEM((nck, ck, tq), F32),
                        pltpu.VMEM((KEY_BITS, L // BLOCK_KEYS, F32_SUBLANES, tq), jnp.int32),
                        pltpu.VMEM((L // BLOCK_KEYS, F32_SUBLANES, tq), jnp.int32),
                        pltpu.VMEM((F32_SUBLANES, tq), jnp.int32),
                        pltpu.VMEM((nck, tq), jnp.int32),
                        pltpu.VMEM((A_HEADS, tq), F32),
                        pltpu.VMEM((A_HEADS, tq), F32),
                        pltpu.VMEM((A_HEADS, DSA_VEXT_ROWS, tq), F32),
                        pltpu.VMEM((ck, A_HEADS * tq), BF16)],
        compiler_params=_params("arbitrary", "arbitrary"),
        name="dsa",
    )(qit, wit, qat, ga, ki, ka, vat)


def _gla_kernel(q_ref, k_ref, v_ref, gb_ref, small_ref, wup_ref, bg_ref, gh_ref, o_ref, s_ref,
                *, c, sub, band):
    @pl.when(pl.program_id(1) == 0)
    def _():
        s_ref[...] = jnp.zeros(s_ref.shape, F32)

    hi = lax.Precision.HIGHEST
    row = lax.broadcasted_iota(jnp.int32, (c, c), 0)
    col = lax.broadcasted_iota(jnp.int32, (c, c), 1)
    tri = jnp.where(col <= row, 1.0, 0.0)
    lane = lax.broadcasted_iota(jnp.int32, (B_KEY_DIM, c), 1)
    krow = lax.broadcasted_iota(jnp.int32, (c, B_KEY_DIM), 0)
    diff = col - row
    ones = jnp.ones((B_KEY_DIM, c), BF16)
    small = small_ref[...]
    halves = []
    half = sub // 2
    while half >= band:
        halves.append((half, (row // (2 * half)) == (col // (2 * half))))
        half //= 2

    heads = range(B_HEADS)
    kcols = [slice(h * B_KEY_DIM, (h + 1) * B_KEY_DIM) for h in heads]
    vcols = [slice(h * B_VAL_DIM, (h + 1) * B_VAL_DIM) for h in heads]
    q = [q_ref[:, kcols[h]] * (B_KEY_DIM ** -0.5) for h in heads]
    k = [k_ref[:, kcols[h]] for h in heads]
    v = [v_ref[:, vcols[h]] for h in heads]
    xg = [jnp.dot(small, wup_ref[:, kcols[h]], precision=hi, preferred_element_type=F32) + bg_ref[:, kcols[h]]
          for h in heads]
    la = [(jnp.minimum(x, 0.0) - jnp.log(1.0 + jnp.exp(-jnp.abs(x)))) * (LOG2E / GATE_TAU) for x in xg]
    g = [jnp.dot(tri, l, precision=hi, preferred_element_type=F32) for l in la]
    gt = [x.T for x in g]
    kt = [x.T for x in k]

    o = [jnp.dot((q[h] * jnp.exp2(g[h])).astype(BF16), s_ref[h].astype(BF16), preferred_element_type=F32)
         for h in heads]

    a_rows = [[jnp.zeros((sub, c), F32)] for h in heads]
    for i in range(1, c // sub):
        lo = i * sub
        for h in heads:
            expo = jnp.where(lane < lo, gt[h][:, lo - 1:lo] - gt[h], -jnp.inf)
            kti = (kt[h] * jnp.exp2(expo)).astype(BF16)
            qi = (q[h][lo:lo + sub, :] * jnp.exp2(g[h][lo:lo + sub, :] - g[h][lo - 1:lo, :])).astype(BF16)
            a_rows[h].append(jnp.dot(qi, kti, preferred_element_type=F32))
    a = [jnp.concatenate(a_rows[h], axis=0) for h in heads]

    for half, same_block in halves:
        size = 2 * half
        pos = krow % size
        for h in heads:
            g3 = g[h].reshape(c // size, size, B_KEY_DIM)
            ref = jnp.broadcast_to(g3[:, half - 1:half, :], g3.shape).reshape(c, B_KEY_DIM)
            ql = (q[h] * jnp.exp2(jnp.where(pos >= half, g[h] - ref, -jnp.inf))).astype(BF16)
            kl = (k[h] * jnp.exp2(jnp.where(pos < half, ref - g[h], -jnp.inf))).astype(BF16)
            al = lax.dot_general(ql, kl, NT_DIMS, preferred_element_type=F32)
            a[h] = a[h] + jnp.where(same_block, al, 0.0)

    rmod = krow % band
    for d in range(band):
        for h in heads:
            ks = k[h] if d == 0 else pltpu.roll(k[h], d, axis=0)
            gs = g[h] if d == 0 else pltpu.roll(g[h], d, axis=0)
            e = jnp.where(rmod >= d, g[h] - gs, -jnp.inf)
            r = jnp.dot((q[h] * ks * jnp.exp2(e)).astype(BF16), ones, preferred_element_type=F32)
            a[h] = a[h] + jnp.where(diff == -d, r, 0.0)

    for h in heads:
        oh = o[h] + jnp.dot(a[h].astype(BF16), v[h], preferred_element_type=F32)
        glast = gt[h][:, c - 1:c]
        kend = (kt[h] * jnp.exp2(glast - gt[h])).astype(BF16)
        s_ref[h] = jnp.exp2(glast) * s_ref[h] + jnp.dot(kend, v[h], preferred_element_type=F32)
        gb = gb_ref[:, vcols[h]]
        o_ref[:, vcols[h]] = (_rms(oh, gh_ref[...]) * (gb * _sigmoid(gb))).astype(o_ref.dtype)


def _gla(qb, kb, vb, gb, small, wup_pad, b_gate, g_head, B, L):
    c = GLA_CHUNK
    nc = L // c
    row = lambda b, i: (b * nc + i, 0)
    fixed = lambda b, i: (0, 0)
    return pl.pallas_call(
        functools.partial(_gla_kernel, c=c, sub=GLA_SUB, band=GLA_BAND),
        grid=(B, nc),
        in_specs=[pl.BlockSpec((c, B_KEY_WIDTH), row),
                  pl.BlockSpec((c, B_KEY_WIDTH), row),
                  pl.BlockSpec((c, B_VAL_WIDTH), row),
                  pl.BlockSpec((c, B_VAL_WIDTH), row),
                  pl.BlockSpec((c, LANES), row),
                  pl.BlockSpec((LANES, B_KEY_WIDTH), fixed),
                  pl.BlockSpec((1, B_KEY_WIDTH), fixed),
                  pl.BlockSpec((1, B_VAL_DIM), fixed)],
        out_specs=pl.BlockSpec((c, B_VAL_WIDTH), row),
        out_shape=jax.ShapeDtypeStruct((B * L, B_VAL_WIDTH), BF16),
        scratch_shapes=[pltpu.VMEM((B_HEADS, B_KEY_DIM, B_VAL_DIM), F32)],
        compiler_params=_params("arbitrary", "arbitrary"),
        name="gla",
    )(qb, kb, vb, gb, small, wup_pad, b_gate.reshape(1, B_KEY_WIDTH), g_head.reshape(1, B_VAL_DIM))


def _merge_kernel(x_ref, p_ref, oa_ref, ob_ref, ma_ref, mb_ref, wpa_ref, wpb_ref, wout_ref,
                  gpost_ref, wple_ref, wpg_ref, gpre_ref, gpost2_ref, o_ref, *, parts):
    rows = x_ref.shape[0] // parts
    sl = [slice(i * rows, (i + 1) * rows) for i in range(parts)]
    dot = lambda a, w_ref: jnp.dot(a, w_ref[...], preferred_element_type=F32)
    ya = [dot(oa_ref[r, :], wpa_ref) for r in sl]
    yb = [dot(ob_ref[r, :], wpb_ref) for r in sl]
    y = [_sigmoid(ma_ref[r, :]) * a + _sigmoid(mb_ref[r, :]) * b for r, a, b in zip(sl, ya, yb)]
    u = [dot(t.astype(BF16), wout_ref) for t in y]
    x1 = [x_ref[r, :] + _rms(t, gpost_ref[...]) for r, t in zip(sl, u)]
    gate = [dot(_rms(t, gpre_ref[...]).astype(BF16), wpg_ref) for t in x1]
    e = [dot(p_ref[r, :].astype(BF16), wple_ref) * _sigmoid(t) for r, t in zip(sl, gate)]
    for r, a, b in zip(sl, x1, e):
        o_ref[r, :] = a + _rms(b, gpost2_ref[...])


def _merge(x2d, p2d, oa, ob, ma, mb, wpa, wpb, wout, g_post, wple, wpg, g_pre2, g_post2):
    T = x2d.shape[0]
    tm = MERGE_ROWS
    rows = lambda width: pl.BlockSpec((tm, width), lambda i: (i, 0))
    full = lambda a: pl.BlockSpec(a.shape, lambda i: (0, 0))
    vec = lambda a: a.reshape(1, D_MODEL)
    args = (x2d, p2d, oa, ob, ma, mb, wpa, wpb, wout, vec(g_post), wple, wpg, vec(g_pre2), vec(g_post2))
    in_specs = [rows(D_MODEL), rows(PLE_DIM), rows(A_WIDTH), rows(B_VAL_WIDTH), rows(D_MODEL), rows(D_MODEL)]
    in_specs += [full(a) for a in args[6:]]
    return pl.pallas_call(
        functools.partial(_merge_kernel, parts=MERGE_PARTS),
        grid=(T // tm,),
        in_specs=in_specs,
        out_specs=rows(D_MODEL),
        out_shape=jax.ShapeDtypeStruct((T, D_MODEL), F32),
        compiler_params=_params("arbitrary"),
        name="merge",
    )(*args)


def _split_w_in(w_in):
    offs = [0]
    for s in IN_SPLITS:
        offs.append(offs[-1] + s)
    col = lambda i: w_in[:, offs[i]:offs[i + 1]]
    qa, ka, va, qi, ki, wi, ga, qb, kb, vb, gdown, gb, ma, mb = [col(i) for i in range(len(IN_SPLITS))]
    pad = jnp.zeros((D_MODEL, LANES - GATE_RANK), w_in.dtype)
    small = jnp.concatenate([gdown, pad], axis=1)
    groups = [qa.T, ka, va.T, qi.T, ki, wi.T, small, ga, qb, kb, vb, gb, ma, mb]
    dtypes = [BF16, BF16, BF16, BF16, BF16, F32, F32, F32, F32, F32, BF16, F32, F32, F32]
    transposed = [True, False, True, True, False, True] + [False] * 8
    return [g.astype(BF16) for g in groups], dtypes, transposed


def _layer(x2d, p2d, B, L, g_pre, w_in, w_gate_up, b_gate, g_gla_head, w_proj_a, w_proj_b,
           w_out, g_post, w_ple, w_ple_gate, g_ple_pre, g_ple_post):
    groups, dtypes, transposed = _split_w_in(w_in)
    qat, ka, vat, qit, ki, wit, small, ga, qb, kb, vb, gb, ma, mb = _in_proj(
        x2d, g_pre, groups, dtypes, transposed)
    oa = _dsa(qit, wit, qat, ga, ki, ka, vat, B, L)
    wup_pad = jnp.zeros((LANES, B_KEY_WIDTH), F32).at[SMALL_GD_OFF:SMALL_GD_OFF + GATE_RANK].set(w_gate_up)
    ob = _gla(qb, kb, vb, gb, small, wup_pad, b_gate, g_gla_head, B, L)
    bf = lambda a: a.astype(BF16)
    return _merge(x2d, p2d, oa, ob, ma, mb, bf(w_proj_a), bf(w_proj_b), bf(w_out), g_post,
                  bf(w_ple), bf(w_ple_gate), g_ple_pre, g_ple_post)


def kernel(x, p, g_pre, w_in, w_gate_up, b_gate, g_gla_head, w_proj_a, w_proj_b, w_out, g_post,
           w_ple, w_ple_gate, g_ple_pre, g_ple_post):
    B, L, _ = x.shape
    depth = p.shape[0]
    x2d = x.reshape(B * L, D_MODEL)
    for i in range(depth):
        x2d = _layer(x2d, p[i].reshape(B * L, PLE_DIM), B, L, g_pre[i], w_in[i], w_gate_up[i],
                     b_gate[i], g_gla_head[i], w_proj_a[i], w_proj_b[i], w_out[i], g_post[i],
                     w_ple[i], w_ple_gate[i], g_ple_pre[i], g_ple_post[i])
    return x2d.reshape(B, L, D_MODEL)
```

```python
import functools

import jax
import jax.numpy as jnp
from jax import lax
from jax.experimental import pallas as pl
from jax.experimental.pallas import tpu as pltpu

F32 = jnp.float32
BF16 = jnp.bfloat16

D_MODEL = 1024
PLE_DIM = 256
A_HEADS = 8
A_HEAD_DIM = 64
A_WIDTH = A_HEADS * A_HEAD_DIM
A_KV_DIM = 64
IDX_HEADS = 8
IDX_DIM = 64
TOPK_MAX = 256
B_HEADS = 4
B_KEY_DIM = 128
B_VAL_DIM = 256
B_KEY_WIDTH = B_HEADS * B_KEY_DIM
B_VAL_WIDTH = B_HEADS * B_VAL_DIM
GATE_RANK = 16
GATE_TAU = 16.0
EPS = 1e-6

IN_SPLITS = (A_WIDTH, A_KV_DIM, A_KV_DIM, IDX_HEADS * IDX_DIM, IDX_DIM, IDX_HEADS, A_WIDTH,
             B_KEY_WIDTH, B_KEY_WIDTH, B_VAL_WIDTH, GATE_RANK, B_VAL_WIDTH,
             D_MODEL, D_MODEL)

VMEM_LIMIT_BYTES = 56 * 1024 * 1024
LANES = 128
BF16_SUBLANES = 16

SMALL_GD_OFF = 0

IN_PROJ_ROWS = 256
MERGE_ROWS = 512
MERGE_PARTS = 2
DSA_Q_COLS = 256
DSA_KEY_CHUNK = 512
DSA_VEXT_ROWS = A_KV_DIM + BF16_SUBLANES
GLA_CHUNK = 128
GLA_SUB = 16
GLA_BAND = 4

NT_DIMS = (((1,), (1,)), ((), ()))


def _sigmoid(x):
    return 1.0 / (1.0 + jnp.exp(-x))


def _rms(x, g):
    ms = jnp.mean(x * x, axis=-1, keepdims=True)
    return x * lax.rsqrt(ms + EPS) * g


def _params(*sem):
    return pltpu.CompilerParams(dimension_semantics=sem, vmem_limit_bytes=VMEM_LIMIT_BYTES)


def _in_proj_kernel(x_ref, g_ref, *refs, transposed):
    n_out = len(transposed)
    w_refs, o_refs = refs[:n_out], refs[n_out:]
    h = _rms(x_ref[...], g_ref[...]).astype(BF16)
    for w_ref, o_ref, tr in zip(w_refs, o_refs, transposed):
        if tr:
            z = lax.dot_general(w_ref[...], h, NT_DIMS, preferred_element_type=F32)
        else:
            z = jnp.dot(h, w_ref[...], preferred_element_type=F32)
        o_ref[...] = z.astype(o_ref.dtype)


def _in_proj(x2d, g_pre, w_groups, out_dtypes, transposed):
    T = x2d.shape[0]
    tm = IN_PROJ_ROWS
    in_specs = [pl.BlockSpec((tm, D_MODEL), lambda i: (i, 0)),
                pl.BlockSpec((1, D_MODEL), lambda i: (0, 0))]
    in_specs += [pl.BlockSpec(w.shape, lambda i: (0, 0)) for w in w_groups]
    out_specs, out_shape = [], []
    for w, dt, tr in zip(w_groups, out_dtypes, transposed):
        if tr:
            out_specs.append(pl.BlockSpec((w.shape[0], tm), lambda i: (0, i)))
            out_shape.append(jax.ShapeDtypeStruct((w.shape[0], T), dt))
        else:
            out_specs.append(pl.BlockSpec((tm, w.shape[1]), lambda i: (i, 0)))
            out_shape.append(jax.ShapeDtypeStruct((T, w.shape[1]), dt))
    return pl.pallas_call(
        functools.partial(_in_proj_kernel, transposed=tuple(transposed)),
        grid=(T // tm,),
        in_specs=in_specs,
        out_specs=out_specs,
        out_shape=out_shape,
        compiler_params=_params("arbitrary"),
        name="in_proj",
    )(x2d, g_pre.reshape(1, D_MODEL), *w_groups)


LOG2E = 1.4426950408889634
MAX_ROWS = 4 * 8
F32_SUBLANES = 8
KEY_BITS = 32
BLOCK_KEYS = KEY_BITS * F32_SUBLANES
SELECT_VARIANTS = 4
NEG_INF_UKEY = 0x007FFFFF
MIN_HEALTHY_DENOM = 2.0 ** -100


def _bit_transpose(words):
    a = list(words)
    j, m = 16, 0x0000FFFF
    while j:
        mask = jnp.int32(m if m < 2 ** 31 else m - 2 ** 32)
        k = 0
        while k < KEY_BITS:
            t = (a[k] ^ lax.shift_right_logical(a[k + j], jnp.int32(j))) & mask
            a[k] = a[k] ^ t
            a[k + j] = a[k + j] ^ lax.shift_left(t, jnp.int32(j))
            k = (k + j + 1) & ~j
        j >>= 1
        m = (m ^ (m << j)) & 0xFFFFFFFF
    return a


def _dsa_kernel(qit_ref, wit_ref, qat_ref, ga_ref, ki_ref, ka_ref, vat_ref, o_ref,
                qis_ref, qas_ref, kaug_ref, knorm_ref, vext_ref, sc_ref, planes_ref, cand_ref, sel_ref, ties_ref,
                m_ref, alpha_ref, acc_ref, p_ref, *, tq, ck, top_k):
    qb = pl.program_id(1)
    nh = A_HEADS
    idx_scale = (IDX_HEADS ** -0.5) * (IDX_DIM ** -0.5)
    attn_scale = (A_HEAD_DIM ** -0.5) * LOG2E
    n_chunks = vext_ref.shape[0]
    n_blocks = cand_ref.shape[0]
    blocks_per_chunk = ck // BLOCK_KEYS

    @pl.when(qb == 0)
    def _():
        for j in range(n_chunks):
            vext_ref[j, 0:A_KV_DIM, :] = vat_ref[:, j * ck:(j + 1) * ck]
            vext_ref[j, A_KV_DIM:, :] = jnp.ones((DSA_VEXT_ROWS - A_KV_DIM, ck), BF16)
        planes_ref[...] = jnp.zeros(planes_ref.shape, jnp.int32)
        kaug_ref[:, 0:A_KV_DIM] = ka_ref[...]
        one_col = lax.broadcasted_iota(jnp.int32, (kaug_ref.shape[0], LANES - A_KV_DIM), 1) == 0
        kaug_ref[:, A_KV_DIM:] = jnp.where(one_col, 1.0, 0.0).astype(BF16)
        qas_ref[A_HEAD_DIM + BF16_SUBLANES:, :] = jnp.zeros((LANES - A_HEAD_DIM - BF16_SUBLANES, nh * tq), BF16)
        k32 = ka_ref[...].astype(F32)
        knorm_ref[...] = jnp.sqrt(jnp.max(jnp.sum(k32 * k32, axis=1, keepdims=True), axis=0, keepdims=True))

    qa = (qat_ref[...].astype(F32) * attn_scale).astype(BF16)
    q32 = qa.astype(F32).reshape(nh, A_HEAD_DIM, tq)
    bound = jnp.sqrt(jnp.sum(q32 * q32, axis=1)) * knorm_ref[...]
    first = lax.broadcasted_iota(jnp.int32, (BF16_SUBLANES, tq), 0) == 0
    for h in range(nh):
        cols = slice(h * tq, (h + 1) * tq)
        qis_ref[:, cols] = qit_ref[h * IDX_DIM:(h + 1) * IDX_DIM, :]
        qas_ref[0:A_HEAD_DIM, cols] = qa[h * A_HEAD_DIM:(h + 1) * A_HEAD_DIM, :]
        qas_ref[A_HEAD_DIM:A_HEAD_DIM + BF16_SUBLANES, cols] = jnp.where(first, -bound[h:h + 1, :], 0.0).astype(BF16)

    q0 = qb * tq
    kd = q0 // ck
    nkc = kd + 1
    w = wit_ref[...] * idx_scale

    def store_planes(kc, sc):
        bits = lax.bitcast_convert_type(sc, jnp.int32)
        ukey = bits ^ ((bits >> 31) | jnp.int32(-2 ** 31))
        for j in range(sc.shape[0] // BLOCK_KEYS):
            rows = [ukey[(j * KEY_BITS + e) * F32_SUBLANES:(j * KEY_BITS + e + 1) * F32_SUBLANES, :]
                    for e in range(KEY_BITS)]
            planes = _bit_transpose(rows)
            for b in range(KEY_BITS):
                planes_ref[b, kc * blocks_per_chunk + j] = planes[b]

    def chunk_logits(kc):
        kt = ki_ref[pl.ds(pl.multiple_of(kc * ck, ck), ck), :]
        return jnp.dot(kt, qis_ref[...], preferred_element_type=F32)

    def chunk_scores(lg):
        sc = jnp.zeros((lg.shape[0], tq), F32)
        for h in range(nh):
            sc = sc + w[h:h + 1, :] * jnp.maximum(lg[:, h * tq:(h + 1) * tq], 0.0)
        return sc

    def score_body(kc, carry):
        sc_kc = chunk_scores(chunk_logits(kc))
        sc_ref[kc] = sc_kc
        store_planes(kc, sc_kc)
        return carry

    lax.fori_loop(0, kd, score_body, 0)

    lead = q0 - kd * ck + tq

    def diagonal_scores(rows):
        kt = ki_ref[pl.ds(pl.multiple_of(kd * ck, ck), rows), :]
        sc = chunk_scores(jnp.dot(kt, qis_ref[...], preferred_element_type=F32))
        kpos = kd * ck + lax.broadcasted_iota(jnp.int32, (rows, tq), 0)
        qpos = q0 + lax.broadcasted_iota(jnp.int32, (rows, tq), 1)
        sc = jnp.where(kpos <= qpos, sc, -jnp.inf)
        sc_ref[kd, 0:rows, :] = sc
        store_planes(kd, sc)
        if rows < ck:
            sc_ref[kd, rows:ck, :] = jnp.full((ck - rows, tq), -jnp.inf, F32)
            for j in range(rows // BLOCK_KEYS, blocks_per_chunk):
                for b in range(KEY_BITS):
                    fill = -((NEG_INF_UKEY >> (KEY_BITS - 1 - b)) & 1)
                    planes_ref[b, kd * blocks_per_chunk + j] = jnp.full((F32_SUBLANES, tq), fill, jnp.int32)

    for rows in range(tq, ck + 1, tq):
        pl.when(lead == rows)(functools.partial(diagonal_scores, rows))

    live_blocks = nkc * blocks_per_chunk
    for blk in range(n_blocks):
        cand_ref[blk] = jnp.full((F32_SUBLANES, tq), jnp.where(blk < live_blocks, -1, 0), jnp.int32)

    def select_body(i, carry, blocks):
        need, ukey_thr = carry
        accs = [jnp.zeros((F32_SUBLANES, tq), jnp.int32) for _ in range(4)]
        for blk in range(blocks):
            accs[blk % 4] = accs[blk % 4] + lax.population_count(cand_ref[blk] & planes_ref[i, blk])
        acc = (accs[0] + accs[1]) + (accs[2] + accs[3])
        ones_cnt = jnp.sum(acc, axis=0, keepdims=True)
        take = ones_cnt >= need
        flip = jnp.where(take, 0, -1)
        for blk in range(blocks):
            cand_ref[blk] = cand_ref[blk] & (planes_ref[i, blk] ^ flip)
        need = jnp.where(take, need, need - ones_cnt)
        ukey_thr = ukey_thr | jnp.where(take, lax.shift_left(jnp.int32(1), KEY_BITS - 1 - i), 0)
        return need, ukey_thr

    init = (jnp.full((1, tq), top_k, jnp.int32), jnp.zeros((1, tq), jnp.int32))
    covered = 0
    for blocks in range(n_blocks // SELECT_VARIANTS, n_blocks + 1, n_blocks // SELECT_VARIANTS):
        @pl.when((live_blocks > covered) & (live_blocks <= blocks))
        def _(blocks=blocks):
            found = lax.fori_loop(0, KEY_BITS, functools.partial(select_body, blocks=blocks), init)
            sel_ref[0:1, :] = found[0]
            sel_ref[1:2, :] = found[1]
        covered = blocks
    need = sel_ref[0:1, :]
    ukey_thr = sel_ref[1:2, :]
    thr_bits = jnp.where(ukey_thr < 0, ukey_thr ^ jnp.int32(-2 ** 31), ~ukey_thr)
    thr = lax.bitcast_convert_type(thr_bits, F32)
    qrow = q0 + lax.broadcasted_iota(jnp.int32, (1, tq), 1)
    thr = jnp.where(qrow < top_k, jnp.finfo(F32).min, thr)

    tied = jnp.zeros((F32_SUBLANES, tq), jnp.int32)
    for blk in range(n_blocks):
        tied = tied + lax.population_count(cand_ref[blk])
    surplus = (jnp.sum(tied, axis=0, keepdims=True) > need) & (qrow >= top_k)

    @pl.when(jnp.max(jnp.where(surplus, 1, 0)) > 0)
    def _():
        surplus_bits = jnp.where(surplus, -1, 0)
        for c in range(n_chunks):
            cnt = jnp.zeros((F32_SUBLANES, tq), jnp.int32)
            for j in range(blocks_per_chunk):
                cnt = cnt + lax.population_count(cand_ref[c * blocks_per_chunk + j] & surplus_bits)
            ties_ref[c:c + 1, :] = jnp.sum(cnt, axis=0, keepdims=True)
        any_tie = jnp.max(ties_ref[...], axis=1, keepdims=True) > 0
        chunk_bit = lax.shift_left(1, lax.broadcasted_iota(jnp.int32, (n_chunks, 1), 0))
        chunk_mask = jnp.sum(jnp.where(any_tie, chunk_bit, 0))

        key_row = lax.broadcasted_iota(jnp.int32, (ck, ck), 0)
        key_col = lax.broadcasted_iota(jnp.int32, (ck, ck), 1)
        upto = jnp.where(key_col <= key_row, 1.0, 0.0).astype(BF16)
        need_f = need.astype(F32)

        def drop_body(kc, seen):
            @pl.when((lax.shift_right_logical(chunk_mask, kc) & 1) == 1)
            def _():
                sc = sc_ref[kc]
                tie = jnp.where((sc == thr) & surplus, 1.0, 0.0)
                rank = jnp.dot(upto, tie.astype(BF16), preferred_element_type=F32) + seen
                sc_ref[kc] = jnp.where((tie > 0.0) & (rank > need_f), -jnp.inf, sc)

            return seen + ties_ref[pl.ds(kc, 1), :].astype(F32)

        lax.fori_loop(0, nkc, drop_body, jnp.zeros((1, tq), F32))

    acc_ref[...] = jnp.zeros(acc_ref.shape, F32)

    def fast_chunk(kc, rows):
        off = pl.multiple_of(kc * ck, ck)
        s = jnp.dot(kaug_ref[pl.ds(off, rows), :], qas_ref[...], preferred_element_type=F32)
        neg = jnp.where(sc_ref[kc, 0:rows, :] >= thr, 0.0, -jnp.inf)
        for h in range(nh):
            cols = slice(h * tq, (h + 1) * tq)
            p_ref[0:rows, cols] = jnp.exp2(s[:, cols] + neg).astype(BF16)
        pv = jnp.dot(vext_ref[kc, :, 0:rows], p_ref[0:rows, :], preferred_element_type=F32)
        for h in range(nh):
            acc_ref[h] = acc_ref[h] + pv[:, h * tq:(h + 1) * tq]

    def fast_body(kc, carry):
        fast_chunk(kc, ck)
        return carry

    lax.fori_loop(0, kd, fast_body, 0)
    for rows in range(tq, ck + 1, tq):
        pl.when(lead == rows)(functools.partial(fast_chunk, kd, rows))

    denom = acc_ref[0, A_KV_DIM:A_KV_DIM + F32_SUBLANES, :]
    for h in range(1, nh):
        denom = jnp.minimum(denom, acc_ref[h, A_KV_DIM:A_KV_DIM + F32_SUBLANES, :])
    healthy = jnp.min(denom) >= MIN_HEALTHY_DENOM

    @pl.when(jnp.logical_not(healthy))
    def _():
        m_ref[...] = jnp.full(m_ref.shape, jnp.finfo(F32).min, F32)
        acc_ref[...] = jnp.zeros(acc_ref.shape, F32)

        def attn_body(kc, carry):
            off = pl.multiple_of(kc * ck, ck)
            s = jnp.dot(ka_ref[pl.ds(off, ck), :], qas_ref[0:A_HEAD_DIM, :], preferred_element_type=F32)
            neg = jnp.where(sc_ref[kc] >= thr, 0.0, -jnp.inf)
            for h in range(nh):
                cols = slice(h * tq, (h + 1) * tq)
                sh = s[:, cols] + neg
                cm = jnp.max(sh.reshape(ck // MAX_ROWS, MAX_ROWS, tq), axis=0)
                m_old = m_ref[h:h + 1, :]
                m_new = jnp.maximum(m_old, jnp.max(cm, axis=0, keepdims=True))
                alpha_ref[h:h + 1, :] = jnp.exp2(m_old - m_new)
                m_ref[h:h + 1, :] = m_new
                p_ref[:, cols] = jnp.exp2(sh - m_new).astype(BF16)
            pv = jnp.dot(vext_ref[kc], p_ref[...], preferred_element_type=F32)
            for h in range(nh):
                acc_ref[h] = alpha_ref[h:h + 1, :] * acc_ref[h] + pv[:, h * tq:(h + 1) * tq]
            return carry

        lax.fori_loop(0, nkc, attn_body, 0)

    o_t = []
    for h in range(nh):
        a = acc_ref[h]
        o_t.append(a[0:A_KV_DIM, :] / a[A_KV_DIM:A_KV_DIM + 1, :])
    o = jnp.concatenate(o_t, axis=0).T
    g = ga_ref[...]
    o_ref[...] = (o * (g * _sigmoid(g))).astype(o_ref.dtype)


def _dsa(qit, wit, qat, ga, ki, ka, vat, B, L):
    tq, ck = DSA_Q_COLS, min(DSA_KEY_CHUNK, L)
    top_k = min(TOPK_MAX, L // 4)
    nq = L // tq
    nck = L // ck
    assert L % ck == 0 and ck % tq == 0 and tq % BLOCK_KEYS == 0 and (L // BLOCK_KEYS) % SELECT_VARIANTS == 0
    col = lambda b, q: (0, b * nq + q)
    kernel = functools.partial(_dsa_kernel, tq=tq, ck=ck, top_k=top_k)
    return pl.pallas_call(
        kernel,
        grid=(B, nq),
        in_specs=[pl.BlockSpec((A_WIDTH, tq), col),
                  pl.BlockSpec((IDX_HEADS, tq), col),
                  pl.BlockSpec((A_WIDTH, tq), col),
                  pl.BlockSpec((tq, A_WIDTH), lambda b, q: (b * nq + q, 0)),
                  pl.BlockSpec((L, IDX_DIM), lambda b, q: (b, 0)),
                  pl.BlockSpec((L, A_KV_DIM), lambda b, q: (b, 0)),
                  pl.BlockSpec((A_KV_DIM, L), lambda b, q: (0, b))],
        out_specs=pl.BlockSpec((tq, A_WIDTH), lambda b, q: (b * nq + q, 0)),
        out_shape=jax.ShapeDtypeStruct((B * L, A_WIDTH), BF16),
        scratch_shapes=[pltpu.VMEM((IDX_DIM, A_HEADS * tq), BF16),
                        pltpu.VMEM((LANES, A_HEADS * tq), BF16),
                        pltpu.VMEM((L, LANES), BF16),
                        pltpu.VMEM((1, 1), F32),
                        pltpu.VMEM((nck, DSA_VEXT_ROWS, ck), BF16),
                        pltpu.VMEM((nck, ck, tq), F32),
                        pltpu.VMEM((KEY_BITS, L // BLOCK_KEYS, F32_SUBLANES, tq), jnp.int32),
                        pltpu.VMEM((L // BLOCK_KEYS, F32_SUBLANES, tq), jnp.int32),
                        pltpu.VMEM((F32_SUBLANES, tq), jnp.int32),
                        pltpu.VMEM((nck, tq), jnp.int32),
                        pltpu.VMEM((A_HEADS, tq), F32),
                        pltpu.VMEM((A_HEADS, tq), F32),
                        pltpu.VMEM((A_HEADS, DSA_VEXT_ROWS, tq), F32),
                        pltpu.VMEM((ck, A_HEADS * tq), BF16)],
        compiler_params=_params("arbitrary", "arbitrary"),
        name="dsa",
    )(qit, wit, qat, ga, ki, ka, vat)


def _gla_kernel(q_ref, k_ref, v_ref, gb_ref, small_ref, wup_ref, bg_ref, gh_ref, o_ref, s_ref,
                *, c, sub, band):
    @pl.when(pl.program_id(1) == 0)
    def _():
        s_ref[...] = jnp.zeros(s_ref.shape, F32)

    hi = lax.Precision.HIGHEST
    row = lax.broadcasted_iota(jnp.int32, (c, c), 0)
    col = lax.broadcasted_iota(jnp.int32, (c, c), 1)
    tri = jnp.where(col <= row, 1.0, 0.0)
    lane = lax.broadcasted_iota(jnp.int32, (B_KEY_DIM, c), 1)
    krow = lax.broadcasted_iota(jnp.int32, (c, B_KEY_DIM), 0)
    diff = col - row
    ones = jnp.ones((B_KEY_DIM, c), BF16)
    small = small_ref[...]
    halves = []
    half = sub // 2
    while half >= band:
        halves.append((half, (row // (2 * half)) == (col // (2 * half))))
        half //= 2

    heads = range(B_HEADS)
    kcols = [slice(h * B_KEY_DIM, (h + 1) * B_KEY_DIM) for h in heads]
    vcols = [slice(h * B_VAL_DIM, (h + 1) * B_VAL_DIM) for h in heads]
    q = [q_ref[:, kcols[h]] * (B_KEY_DIM ** -0.5) for h in heads]
    k = [k_ref[:, kcols[h]] for h in heads]
    v = [v_ref[:, vcols[h]] for h in heads]
    xg = [jnp.dot(small, wup_ref[:, kcols[h]], precision=hi, preferred_element_type=F32) + bg_ref[:, kcols[h]]
          for h in heads]
    la = [(jnp.minimum(x, 0.0) - jnp.log(1.0 + jnp.exp(-jnp.abs(x)))) * (LOG2E / GATE_TAU) for x in xg]
    g = [jnp.dot(tri, l, precision=hi, preferred_element_type=F32) for l in la]
    gt = [x.T for x in g]
    kt = [x.T for x in k]

    o = [jnp.dot((q[h] * jnp.exp2(g[h])).astype(BF16), s_ref[h].astype(BF16), preferred_element_type=F32)
         for h in heads]

    a_rows = [[jnp.zeros((sub, c), F32)] for h in heads]
    for i in range(1, c // sub):
        lo = i * sub
        for h in heads:
            expo = jnp.where(lane < lo, gt[h][:, lo - 1:lo] - gt[h], -jnp.inf)
            kti = (kt[h] * jnp.exp2(expo)).astype(BF16)
            qi = (q[h][lo:lo + sub, :] * jnp.exp2(g[h][lo:lo + sub, :] - g[h][lo - 1:lo, :])).astype(BF16)
            a_rows[h].append(jnp.dot(qi, kti, preferred_element_type=F32))
    a = [jnp.concatenate(a_rows[h], axis=0) for h in heads]

    for half, same_block in halves:
        size = 2 * half
        pos = krow % size
        for h in heads:
            g3 = g[h].reshape(c // size, size, B_KEY_DIM)
            ref = jnp.broadcast_to(g3[:, half - 1:half, :], g3.shape).reshape(c, B_KEY_DIM)
            ql = (q[h] * jnp.exp2(jnp.where(pos >= half, g[h] - ref, -jnp.inf))).astype(BF16)
            kl = (k[h] * jnp.exp2(jnp.where(pos < half, ref - g[h], -jnp.inf))).astype(BF16)
            al = lax.dot_general(ql, kl, NT_DIMS, preferred_element_type=F32)
            a[h] = a[h] + jnp.where(same_block, al, 0.0)

    rmod = krow % band
    for d in range(band):
        for h in heads:
            ks = k[h] if d == 0 else pltpu.roll(k[h], d, axis=0)
            gs = g[h] if d == 0 else pltpu.roll(g[h], d, axis=0)
            e = jnp.where(rmod >= d, g[h] - gs, -jnp.inf)
            r = jnp.dot((q[h] * ks * jnp.exp2(e)).astype(BF16), ones, preferred_element_type=F32)
            a[h] = a[h] + jnp.where(diff == -d, r, 0.0)

    for h in heads:
        oh = o[h] + jnp.dot(a[h].astype(BF16), v[h], preferred_element_type=F32)
        glast = gt[h][:, c - 1:c]
        kend = (kt[h] * jnp.exp2(glast - gt[h])).astype(BF16)
        s_ref[h] = jnp.exp2(glast) * s_ref[h] + jnp.dot(kend, v[h], preferred_element_type=F32)
        gb = gb_ref[:, vcols[h]]
        o_ref[:, vcols[h]] = (_rms(oh, gh_ref[...]) * (gb * _sigmoid(gb))).astype(o_ref.dtype)


def _gla(qb, kb, vb, gb, small, wup_pad, b_gate, g_head, B, L):
    c = GLA_CHUNK
    nc = L // c
    row = lambda b, i: (b * nc + i, 0)
    fixed = lambda b, i: (0, 0)
    return pl.pallas_call(
        functools.partial(_gla_kernel, c=c, sub=GLA_SUB, band=GLA_BAND),
        grid=(B, nc),
        in_specs=[pl.BlockSpec((c, B_KEY_WIDTH), row),
                  pl.BlockSpec((c, B_KEY_WIDTH), row),
                  pl.BlockSpec((c, B_VAL_WIDTH), row),
                  pl.BlockSpec((c, B_VAL_WIDTH), row),
                  pl.BlockSpec((c, LANES), row),
                  pl.BlockSpec((LANES, B_KEY_WIDTH), fixed),
                  pl.BlockSpec((1, B_KEY_WIDTH), fixed),
                  pl.BlockSpec((1, B_VAL_DIM), fixed)],
        out_specs=pl.BlockSpec((c, B_VAL_WIDTH), row),
        out_shape=jax.ShapeDtypeStruct((B * L, B_VAL_WIDTH), BF16),
        scratch_shapes=[pltpu.VMEM((B_HEADS, B_KEY_DIM, B_VAL_DIM), F32)],
        compiler_params=_params("arbitrary", "arbitrary"),
        name="gla",
    )(qb, kb, vb, gb, small, wup_pad, b_gate.reshape(1, B_KEY_WIDTH), g_head.reshape(1, B_VAL_DIM))


def _merge_kernel(x_ref, p_ref, oa_ref, ob_ref, ma_ref, mb_ref, wpa_ref, wpb_ref, wout_ref,
                  gpost_ref, wple_ref, wpg_ref, gpre_ref, gpost2_ref, o_ref, *, parts):
    rows = x_ref.shape[0] // parts
    sl = [slice(i * rows, (i + 1) * rows) for i in range(parts)]
    dot = lambda a, w_ref: jnp.dot(a, w_ref[...], preferred_element_type=F32)
    ya = [dot(oa_ref[r, :], wpa_ref) for r in sl]
    yb = [dot(ob_ref[r, :], wpb_ref) for r in sl]
    y = [_sigmoid(ma_ref[r, :]) * a + _sigmoid(mb_ref[r, :]) * b for r, a, b in zip(sl, ya, yb)]
    u = [dot(t.astype(BF16), wout_ref) for t in y]
    x1 = [x_ref[r, :] + _rms(t, gpost_ref[...]) for r, t in zip(sl, u)]
    gate = [dot(_rms(t, gpre_ref[...]).astype(BF16), wpg_ref) for t in x1]
    e = [dot(p_ref[r, :].astype(BF16), wple_ref) * _sigmoid(t) for r, t in zip(sl, gate)]
    for r, a, b in zip(sl, x1, e):
        o_ref[r, :] = a + _rms(b, gpost2_ref[...])


def _merge(x2d, p2d, oa, ob, ma, mb, wpa, wpb, wout, g_post, wple, wpg, g_pre2, g_post2):
    T = x2d.shape[0]
    tm = MERGE_ROWS
    rows = lambda width: pl.BlockSpec((tm, width), lambda i: (i, 0))
    full = lambda a: pl.BlockSpec(a.shape, lambda i: (0, 0))
    vec = lambda a: a.reshape(1, D_MODEL)
    args = (x2d, p2d, oa, ob, ma, mb, wpa, wpb, wout, vec(g_post), wple, wpg, vec(g_pre2), vec(g_post2))
    in_specs = [rows(D_MODEL), rows(PLE_DIM), rows(A_WIDTH), rows(B_VAL_WIDTH), rows(D_MODEL), rows(D_MODEL)]
    in_specs += [full(a) for a in args[6:]]
    return pl.pallas_call(
        functools.partial(_merge_kernel, parts=MERGE_PARTS),
        grid=(T // tm,),
        in_specs=in_specs,
        out_specs=rows(D_MODEL),
        out_shape=jax.ShapeDtypeStruct((T, D_MODEL), F32),
        compiler_params=_params("arbitrary"),
        name="merge",
    )(*args)


def _split_w_in(w_in):
    offs = [0]
    for s in IN_SPLITS:
        offs.append(offs[-1] + s)
    col = lambda i: w_in[:, offs[i]:offs[i + 1]]
    qa, ka, va, qi, ki, wi, ga, qb, kb, vb, gdown, gb, ma, mb = [col(i) for i in range(len(IN_SPLITS))]
    pad = jnp.zeros((D_MODEL, LANES - GATE_RANK), w_in.dtype)
    small = jnp.concatenate([gdown, pad], axis=1)
    groups = [qa.T, ka, va.T, qi.T, ki, wi.T, small, ga, qb, kb, vb, gb, ma, mb]
    dtypes = [BF16, BF16, BF16, BF16, BF16, F32, F32, F32, F32, F32, BF16, F32, F32, F32]
    transposed = [True, False, True, True, False, True] + [False] * 8
    return [g.astype(BF16) for g in groups], dtypes, transposed


def _layer(x2d, p2d, B, L, g_pre, w_in, w_gate_up, b_gate, g_gla_head, w_proj_a, w_proj_b,
           w_out, g_post, w_ple, w_ple_gate, g_ple_pre, g_ple_post):
    groups, dtypes, transposed = _split_w_in(w_in)
    qat, ka, vat, qit, ki, wit, small, ga, qb, kb, vb, gb, ma, mb = _in_proj(
        x2d, g_pre, groups, dtypes, transposed)
    oa = _dsa(qit, wit, qat, ga, ki, ka, vat, B, L)
    wup_pad = jnp.zeros((LANES, B_KEY_WIDTH), F32).at[SMALL_GD_OFF:SMALL_GD_OFF + GATE_RANK].set(w_gate_up)
    ob = _gla(qb, kb, vb, gb, small, wup_pad, b_gate, g_gla_head, B, L)
    bf = lambda a: a.astype(BF16)
    return _merge(x2d, p2d, oa, ob, ma, mb, bf(w_proj_a), bf(w_proj_b), bf(w_out), g_post,
                  bf(w_ple), bf(w_ple_gate), g_ple_pre, g_ple_post)


def kernel(x, p, g_pre, w_in, w_gate_up, b_gate, g_gla_head, w_proj_a, w_proj_b, w_out, g_post,
           w_ple, w_ple_gate, g_ple_pre, g_ple_post):
    B, L, _ = x.shape
    depth = p.shape[0]
    x2d = x.reshape(B * L, D_MODEL)
    for i in range(depth):
        x2d = _layer(x2d, p[i].reshape(B * L, PLE_DIM), B, L, g_pre[i], w_in[i], w_gate_up[i],
                     b_gate[i], g_gla_head[i], w_proj_a[i], w_proj_b[i], w_out[i], g_post[i],
                     w_ple[i], w_ple_gate[i], g_ple_pre[i], g_ple_post[i])
    return x2d.reshape(B, L, D_MODEL)
```

```python
import functools

import jax
import jax.numpy as jnp
from jax import lax
from jax.experimental import pallas as pl
from jax.experimental.pallas import tpu as pltpu

F32 = jnp.float32
BF16 = jnp.bfloat16

D_MODEL = 1024
PLE_DIM = 256
A_HEADS = 8
A_HEAD_DIM = 64
A_WIDTH = A_HEADS * A_HEAD_DIM
A_KV_DIM = 64
IDX_HEADS = 8
IDX_DIM = 64
TOPK_MAX = 256
B_HEADS = 4
B_KEY_DIM = 128
B_VAL_DIM = 256
B_KEY_WIDTH = B_HEADS * B_KEY_DIM
B_VAL_WIDTH = B_HEADS * B_VAL_DIM
GATE_RANK = 16
GATE_TAU = 16.0
EPS = 1e-6

IN_SPLITS = (A_WIDTH, A_KV_DIM, A_KV_DIM, IDX_HEADS * IDX_DIM, IDX_DIM, IDX_HEADS, A_WIDTH,
             B_KEY_WIDTH, B_KEY_WIDTH, B_VAL_WIDTH, GATE_RANK, B_VAL_WIDTH,
             D_MODEL, D_MODEL)

VMEM_LIMIT_BYTES = 56 * 1024 * 1024
LANES = 128
BF16_SUBLANES = 16

SMALL_GD_OFF = 0

IN_PROJ_ROWS = 256
MERGE_ROWS = 512
MERGE_PARTS = 2
DSA_Q_COLS = 256
DSA_KEY_CHUNK = 512
DSA_VEXT_ROWS = A_KV_DIM + BF16_SUBLANES
GLA_CHUNK = 128
GLA_SUB = 16
GLA_BAND = 4

NT_DIMS = (((1,), (1,)), ((), ()))


def _sigmoid(x):
    return 1.0 / (1.0 + jnp.exp(-x))


def _rms(x, g):
    ms = jnp.mean(x * x, axis=-1, keepdims=True)
    return x * lax.rsqrt(ms + EPS) * g


def _params(*sem):
    return pltpu.CompilerParams(dimension_semantics=sem, vmem_limit_bytes=VMEM_LIMIT_BYTES)


def _in_proj_kernel(x_ref, g_ref, *refs, transposed):
    n_out = len(transposed)
    w_refs, o_refs = refs[:n_out], refs[n_out:]
    h = _rms(x_ref[...], g_ref[...]).astype(BF16)
    for w_ref, o_ref, tr in zip(w_refs, o_refs, transposed):
        if tr:
            z = lax.dot_general(w_ref[...], h, NT_DIMS, preferred_element_type=F32)
        else:
            z = jnp.dot(h, w_ref[...], preferred_element_type=F32)
        o_ref[...] = z.astype(o_ref.dtype)


def _in_proj(x2d, g_pre, w_groups, out_dtypes, transposed):
    T = x2d.shape[0]
    tm = IN_PROJ_ROWS
    in_specs = [pl.BlockSpec((tm, D_MODEL), lambda i: (i, 0)),
                pl.BlockSpec((1, D_MODEL), lambda i: (0, 0))]
    in_specs += [pl.BlockSpec(w.shape, lambda i: (0, 0)) for w in w_groups]
    out_specs, out_shape = [], []
    for w, dt, tr in zip(w_groups, out_dtypes, transposed):
        if tr:
            out_specs.append(pl.BlockSpec((w.shape[0], tm), lambda i: (0, i)))
            out_shape.append(jax.ShapeDtypeStruct((w.shape[0], T), dt))
        else:
            out_specs.append(pl.BlockSpec((tm, w.shape[1]), lambda i: (i, 0)))
            out_shape.append(jax.ShapeDtypeStruct((T, w.shape[1]), dt))
    return pl.pallas_call(
        functools.partial(_in_proj_kernel, transposed=tuple(transposed)),
        grid=(T // tm,),
        in_specs=in_specs,
        out_specs=out_specs,
        out_shape=out_shape,
        compiler_params=_params("arbitrary"),
        name="in_proj",
    )(x2d, g_pre.reshape(1, D_MODEL), *w_groups)


LOG2E = 1.4426950408889634
MAX_ROWS = 4 * 8
F32_SUBLANES = 8
KEY_BITS = 32
BLOCK_KEYS = KEY_BITS * F32_SUBLANES
SELECT_VARIANTS = 4
NEG_INF_UKEY = 0x007FFFFF
MIN_HEALTHY_DENOM = 2.0 ** -100


def _bit_transpose(words):
    a = list(words)
    j, m = 16, 0x0000FFFF
    while j:
        mask = jnp.int32(m if m < 2 ** 31 else m - 2 ** 32)
        k = 0
        while k < KEY_BITS:
            t = (a[k] ^ lax.shift_right_logical(a[k + j], jnp.int32(j))) & mask
            a[k] = a[k] ^ t
            a[k + j] = a[k + j] ^ lax.shift_left(t, jnp.int32(j))
            k = (k + j + 1) & ~j
        j >>= 1
        m = (m ^ (m << j)) & 0xFFFFFFFF
    return a


def _dsa_kernel(qit_ref, wit_ref, qat_ref, ga_ref, ki_ref, ka_ref, vat_ref, o_ref,
                qis_ref, qas_ref, kaug_ref, knorm_ref, vext_ref, sc_ref, planes_ref, cand_ref, sel_ref, ties_ref,
                m_ref, alpha_ref, acc_ref, p_ref, *, tq, ck, top_k):
    qb = pl.program_id(1)
    nh = A_HEADS
    idx_scale = (IDX_HEADS ** -0.5) * (IDX_DIM ** -0.5)
    attn_scale = (A_HEAD_DIM ** -0.5) * LOG2E
    n_chunks = vext_ref.shape[0]
    n_blocks = cand_ref.shape[0]
    blocks_per_chunk = ck // BLOCK_KEYS

    @pl.when(qb == 0)
    def _():
        for j in range(n_chunks):
            vext_ref[j, 0:A_KV_DIM, :] = vat_ref[:, j * ck:(j + 1) * ck]
            vext_ref[j, A_KV_DIM:, :] = jnp.ones((DSA_VEXT_ROWS - A_KV_DIM, ck), BF16)
        planes_ref[...] = jnp.zeros(planes_ref.shape, jnp.int32)
        kaug_ref[:, 0:A_KV_DIM] = ka_ref[...]
        one_col = lax.broadcasted_iota(jnp.int32, (kaug_ref.shape[0], LANES - A_KV_DIM), 1) == 0
        kaug_ref[:, A_KV_DIM:] = jnp.where(one_col, 1.0, 0.0).astype(BF16)
        qas_ref[A_HEAD_DIM + BF16_SUBLANES:, :] = jnp.zeros((LANES - A_HEAD_DIM - BF16_SUBLANES, nh * tq), BF16)
        k32 = ka_ref[...].astype(F32)
        knorm_ref[...] = jnp.sqrt(jnp.max(jnp.sum(k32 * k32, axis=1, keepdims=True), axis=0, keepdims=True))

    qa = (qat_ref[...].astype(F32) * attn_scale).astype(BF16)
    q32 = qa.astype(F32).reshape(nh, A_HEAD_DIM, tq)
    bound = jnp.sqrt(jnp.sum(q32 * q32, axis=1)) * knorm_ref[...]
    first = lax.broadcasted_iota(jnp.int32, (BF16_SUBLANES, tq), 0) == 0
    for h in range(nh):
        cols = slice(h * tq, (h + 1) * tq)
        qis_ref[:, cols] = qit_ref[h * IDX_DIM:(h + 1) * IDX_DIM, :]
        qas_ref[0:A_HEAD_DIM, cols] = qa[h * A_HEAD_DIM:(h + 1) * A_HEAD_DIM, :]
        qas_ref[A_HEAD_DIM:A_HEAD_DIM + BF16_SUBLANES, cols] = jnp.where(first, -bound[h:h + 1, :], 0.0).astype(BF16)

    q0 = qb * tq
    kd = q0 // ck
    nkc = kd + 1
    w = wit_ref[...] * idx_scale

    def store_planes(kc, sc):
        bits = lax.bitcast_convert_type(sc, jnp.int32)
        ukey = bits ^ ((bits >> 31) | jnp.int32(-2 ** 31))
        for j in range(sc.shape[0] // BLOCK_KEYS):
            rows = [ukey[(j * KEY_BITS + e) * F32_SUBLANES:(j * KEY_BITS + e + 1) * F32_SUBLANES, :]
                    for e in range(KEY_BITS)]
            planes = _bit_transpose(rows)
            for b in range(KEY_BITS):
                planes_ref[b, kc * blocks_per_chunk + j] = planes[b]

    def chunk_logits(kc):
        kt = ki_ref[pl.ds(pl.multiple_of(kc * ck, ck), ck), :]
        return jnp.dot(kt, qis_ref[...], preferred_element_type=F32)

    def chunk_scores(lg):
        sc = jnp.zeros((lg.shape[0], tq), F32)
        for h in range(nh):
            sc = sc + w[h:h + 1, :] * jnp.maximum(lg[:, h * tq:(h + 1) * tq], 0.0)
        return sc

    def score_body(kc, carry):
        sc_kc = chunk_scores(chunk_logits(kc))
        sc_ref[kc] = sc_kc
        store_planes(kc, sc_kc)
        return carry

    lax.fori_loop(0, kd, score_body, 0)

    lead = q0 - kd * ck + tq

    def diagonal_scores(rows):
        kt = ki_ref[pl.ds(pl.multiple_of(kd * ck, ck), rows), :]
        sc = chunk_scores(jnp.dot(kt, qis_ref[...], preferred_element_type=F32))
        kpos = kd * ck + lax.broadcasted_iota(jnp.int32, (rows, tq), 0)
        qpos = q0 + lax.broadcasted_iota(jnp.int32, (rows, tq), 1)
        sc = jnp.where(kpos <= qpos, sc, -jnp.inf)
        sc_ref[kd, 0:rows, :] = sc
        store_planes(kd, sc)
        if rows < ck:
            sc_ref[kd, rows:ck, :] = jnp.full((ck - rows, tq), -jnp.inf, F32)
            for j in range(rows // BLOCK_KEYS, blocks_per_chunk):
                for b in range(KEY_BITS):
                    fill = -((NEG_INF_UKEY >> (KEY_BITS - 1 - b)) & 1)
                    planes_ref[b, kd * blocks_per_chunk + j] = jnp.full((F32_SUBLANES, tq), fill, jnp.int32)

    for rows in range(tq, ck + 1, tq):
        pl.when(lead == rows)(functools.partial(diagonal_scores, rows))

    live_blocks = nkc * blocks_per_chunk
    for blk in range(n_blocks):
        cand_ref[blk] = jnp.full((F32_SUBLANES, tq), jnp.where(blk < live_blocks, -1, 0), jnp.int32)

    def select_body(i, carry, blocks):
        need, ukey_thr = carry
        accs = [jnp.zeros((F32_SUBLANES, tq), jnp.int32) for _ in range(4)]
        for blk in range(blocks):
            accs[blk % 4] = accs[blk % 4] + lax.population_count(cand_ref[blk] & planes_ref[i, blk])
        acc = (accs[0] + accs[1]) + (accs[2] + accs[3])
        ones_cnt = jnp.sum(acc, axis=0, keepdims=True)
        take = ones_cnt >= need
        flip = jnp.where(take, 0, -1)
        for blk in range(blocks):
            cand_ref[blk] = cand_ref[blk] & (planes_ref[i, blk] ^ flip)
        need = jnp.where(take, need, need - ones_cnt)
        ukey_thr = ukey_thr | jnp.where(take, lax.shift_left(jnp.int32(1), KEY_BITS - 1 - i), 0)
        return need, ukey_thr

    init = (jnp.full((1, tq), top_k, jnp.int32), jnp.zeros((1, tq), jnp.int32))
    covered = 0
    for blocks in range(n_blocks // SELECT_VARIANTS, n_blocks + 1, n_blocks // SELECT_VARIANTS):
        @pl.when((live_blocks > covered) & (live_blocks <= blocks))
        def _(blocks=blocks):
            found = lax.fori_loop(0, KEY_BITS, functools.partial(select_body, blocks=blocks), init)
            sel_ref[0:1, :] = found[0]
            sel_ref[1:2, :] = found[1]
        covered = blocks
    need = sel_ref[0:1, :]
    ukey_thr = sel_ref[1:2, :]
    thr_bits = jnp.where(ukey_thr < 0, ukey_thr ^ jnp.int32(-2 ** 31), ~ukey_thr)
    thr = lax.bitcast_convert_type(thr_bits, F32)
    qrow = q0 + lax.broadcasted_iota(jnp.int32, (1, tq), 1)
    thr = jnp.where(qrow < top_k, jnp.finfo(F32).min, thr)

    tied = jnp.zeros((F32_SUBLANES, tq), jnp.int32)
    for blk in range(n_blocks):
        tied = tied + lax.population_count(cand_ref[blk])
    surplus = (jnp.sum(tied, axis=0, keepdims=True) > need) & (qrow >= top_k)

    @pl.when(jnp.max(jnp.where(surplus, 1, 0)) > 0)
    def _():
        surplus_bits = jnp.where(surplus, -1, 0)
        for c in range(n_chunks):
            cnt = jnp.zeros((F32_SUBLANES, tq), jnp.int32)
            for j in range(blocks_per_chunk):
                cnt = cnt + lax.population_count(cand_ref[c * blocks_per_chunk + j] & surplus_bits)
            ties_ref[c:c + 1, :] = jnp.sum(cnt, axis=0, keepdims=True)
        any_tie = jnp.max(ties_ref[...], axis=1, keepdims=True) > 0
        chunk_bit = lax.shift_left(1, lax.broadcasted_iota(jnp.int32, (n_chunks, 1), 0))
        chunk_mask = jnp.sum(jnp.where(any_tie, chunk_bit, 0))

        key_row = lax.broadcasted_iota(jnp.int32, (ck, ck), 0)
        key_col = lax.broadcasted_iota(jnp.int32, (ck, ck), 1)
        upto = jnp.where(key_col <= key_row, 1.0, 0.0).astype(BF16)
        need_f = need.astype(F32)

        def drop_body(kc, seen):
            @pl.when((lax.shift_right_logical(chunk_mask, kc) & 1) == 1)
            def _():
                sc = sc_ref[kc]
                tie = jnp.where((sc == thr) & surplus, 1.0, 0.0)
                rank = jnp.dot(upto, tie.astype(BF16), preferred_element_type=F32) + seen
                sc_ref[kc] = jnp.where((tie > 0.0) & (rank > need_f), -jnp.inf, sc)

            return seen + ties_ref[pl.ds(kc, 1), :].astype(F32)

        lax.fori_loop(0, nkc, drop_body, jnp.zeros((1, tq), F32))

    acc_ref[...] = jnp.zeros(acc_ref.shape, F32)

    def fast_chunk(kc, rows):
        off = pl.multiple_of(kc * ck, ck)
        s = jnp.dot(kaug_ref[pl.ds(off, rows), :], qas_ref[...], preferred_element_type=F32)
        neg = jnp.where(sc_ref[kc, 0:rows, :] >= thr, 0.0, -jnp.inf)
        for h in range(nh):
            cols = slice(h * tq, (h + 1) * tq)
            p_ref[0:rows, cols] = jnp.exp2(s[:, cols] + neg).astype(BF16)
        pv = jnp.dot(vext_ref[kc, :, 0:rows], p_ref[0:rows, :], preferred_element_type=F32)
        for h in range(nh):
            acc_ref[h] = acc_ref[h] + pv[:, h * tq:(h + 1) * tq]

    def fast_body(kc, carry):
        fast_chunk(kc, ck)
        return carry

    lax.fori_loop(0, kd, fast_body, 0)
    for rows in range(tq, ck + 1, tq):
        pl.when(lead == rows)(functools.partial(fast_chunk, kd, rows))

    denom = acc_ref[0, A_KV_DIM:A_KV_DIM + F32_SUBLANES, :]
    for h in range(1, nh):
        denom = jnp.minimum(denom, acc_ref[h, A_KV_DIM:A_KV_DIM + F32_SUBLANES, :])
    healthy = jnp.min(denom) >= MIN_HEALTHY_DENOM

    @pl.when(jnp.logical_not(healthy))
    def _():
        m_ref[...] = jnp.full(m_ref.shape, jnp.finfo(F32).min, F32)
        acc_ref[...] = jnp.zeros(acc_ref.shape, F32)

        def attn_body(kc, carry):
            off = pl.multiple_of(kc * ck, ck)
            s = jnp.dot(ka_ref[pl.ds(off, ck), :], qas_ref[0:A_HEAD_DIM, :], preferred_element_type=F32)
            neg = jnp.where(sc_ref[kc] >= thr, 0.0, -jnp.inf)
            for h in range(nh):
                cols = slice(h * tq, (h + 1) * tq)
                sh = s[:, cols] + neg
                cm = jnp.max(sh.reshape(ck // MAX_ROWS, MAX_ROWS, tq), axis=0)
                m_old = m_ref[h:h + 1, :]
                m_new = jnp.maximum(m_old, jnp.max(cm, axis=0, keepdims=True))
                alpha_ref[h:h + 1, :] = jnp.exp2(m_old - m_new)
                m_ref[h:h + 1, :] = m_new
                p_ref[:, cols] = jnp.exp2(sh - m_new).astype(BF16)
            pv = jnp.dot(vext_ref[kc], p_ref[...], preferred_element_type=F32)
            for h in range(nh):
                acc_ref[h] = alpha_ref[h:h + 1, :] * acc_ref[h] + pv[:, h * tq:(h + 1) * tq]
            return carry

        lax.fori_loop(0, nkc, attn_body, 0)

    o_t = []
    for h in range(nh):
        a = acc_ref[h]
        o_t.append(a[0:A_KV_DIM, :] / a[A_KV_DIM:A_KV_DIM + 1, :])
    o = jnp.concatenate(o_t, axis=0).T
    g = ga_ref[...]
    o_ref[...] = (o * (g * _sigmoid(g))).astype(o_ref.dtype)


def _dsa(qit, wit, qat, ga, ki, ka, vat, B, L):
    tq, ck = DSA_Q_COLS, min(DSA_KEY_CHUNK, L)
    top_k = min(TOPK_MAX, L // 4)
    nq = L // tq
    nck = L // ck
    assert L % ck == 0 and ck % BLOCK_KEYS == 0 and (L // BLOCK_KEYS) % SELECT_VARIANTS == 0
    col = lambda b, q: (0, b * nq + q)
    kernel = functools.partial(_dsa_kernel, tq=tq, ck=ck, top_k=top_k)
    return pl.pallas_call(
        kernel,
        grid=(B, nq),
        in_specs=[pl.BlockSpec((A_WIDTH, tq), col),
                  pl.BlockSpec((IDX_HEADS, tq), col),
                  pl.BlockSpec((A_WIDTH, tq), col),
                  pl.BlockSpec((tq, A_WIDTH), lambda b, q: (b * nq + q, 0)),
                  pl.BlockSpec((L, IDX_DIM), lambda b, q: (b, 0)),
                  pl.BlockSpec((L, A_KV_DIM), lambda b, q: (b, 0)),
                  pl.BlockSpec((A_KV_DIM, L), lambda b, q: (0, b))],
        out_specs=pl.BlockSpec((tq, A_WIDTH), lambda b, q: (b * nq + q, 0)),
        out_shape=jax.ShapeDtypeStruct((B * L, A_WIDTH), BF16),
        scratch_shapes=[pltpu.VMEM((IDX_DIM, A_HEADS * tq), BF16),
                        pltpu.VMEM((LANES, A_HEADS * tq), BF16),
                        pltpu.VMEM((L, LANES), BF16),
                        pltpu.VMEM((1, 1), F32),
                        pltpu.VMEM((nck, DSA_VEXT_ROWS, ck), BF16),
                        pltpu.VMEM((nck, ck, tq), F32),
                        pltpu.VMEM((KEY_BITS, L // BLOCK_KEYS, F32_SUBLANES, tq), jnp.int32),
                        pltpu.VMEM((L // BLOCK_KEYS, F32_SUBLANES, tq), jnp.int32),
                        pltpu.VMEM((F32_SUBLANES, tq), jnp.int32),
                        pltpu.VMEM((nck, tq), jnp.int32),
                        pltpu.VMEM((A_HEADS, tq), F32),
                        pltpu.VMEM((A_HEADS, tq), F32),
                        pltpu.VMEM((A_HEADS, DSA_VEXT_ROWS, tq), F32),
                        pltpu.VMEM((ck, A_HEADS * tq), BF16)],
        compiler_params=_params("arbitrary", "arbitrary"),
        name="dsa",
    )(qit, wit, qat, ga, ki, ka, vat)


def _split_bf16(x, terms):
    parts = []
    for _ in range(terms):
        p = x.astype(BF16)
        parts.append(p)
        x = x - p.astype(F32)
    return parts


def _gla_kernel(q_ref, k_ref, v_ref, gb_ref, small_ref, wup_ref, bg_ref, gh_ref, o_ref, s_ref,
                *, c, sub, band):
    @pl.when(pl.program_id(1) == 0)
    def _():
        s_ref[...] = jnp.zeros(s_ref.shape, F32)

    dot = lambda a, b: jnp.dot(a, b, preferred_element_type=F32)
    row = lax.broadcasted_iota(jnp.int32, (c, c), 0)
    col = lax.broadcasted_iota(jnp.int32, (c, c), 1)
    tri = jnp.where(col <= row, 1.0, 0.0).astype(BF16)
    lane =lax.broadcasted_iota(jnp.int32, (B_KEY_DIM, c), 1)
    krow = lax.broadcasted_iota(jnp.int32, (c, B_KEY_DIM), 0)
    diff = col - row
    ones = jnp.ones((B_KEY_DIM, c), BF16)
    small = small_ref[...]
    halves = []
    half = sub // 2
    while half >= band:
        halves.append((half, (row // (2 * half)) == (col // (2 * half))))
        half //= 2

    heads = range(B_HEADS)
    kcols = [slice(h * B_KEY_DIM, (h + 1) * B_KEY_DIM) for h in heads]
    vcols = [slice(h * B_VAL_DIM, (h + 1) * B_VAL_DIM) for h in heads]
    q = [q_ref[:, kcols[h]] * (B_KEY_DIM ** -0.5) for h in heads]
    k = [k_ref[:, kcols[h]] for h in heads]
    v = [v_ref[:, vcols[h]] for h in heads]
    s_hi, s_lo = _split_bf16(small, 2)
    xg = []
    for h in heads:
        w_hi, w_lo = _split_bf16(wup_ref[:, kcols[h]], 2)
        xg.append(dot(s_hi, w_hi) + (dot(s_hi, w_lo) + dot(s_lo, w_hi)) + bg_ref[:, kcols[h]])
    la = [(jnp.minimum(x, 0.0) - jnp.log(1.0 + jnp.exp(-jnp.abs(x)))) * (LOG2E / GATE_TAU) for x in xg]
    g = []
    for l in la:
        l_hi, l_mid, l_lo = _split_bf16(l, 3)
        g.append(dot(tri, l_hi) + (dot(tri, l_mid) + dot(tri, l_lo)))
    gt = [x.T for x in g]
    kt = [x.T for x in k]

    o = [jnp.dot((q[h] * jnp.exp2(g[h])).astype(BF16), s_ref[h].astype(BF16), preferred_element_type=F32)
         for h in heads]

    a_rows = [[jnp.zeros((sub, c), F32)] for h in heads]
    for i in range(1, c // sub):
        lo = i * sub
        for h in heads:
            expo = jnp.where(lane < lo, gt[h][:, lo - 1:lo] - gt[h], -jnp.inf)
            kti = (kt[h] * jnp.exp2(expo)).astype(BF16)
            qi = (q[h][lo:lo + sub, :] * jnp.exp2(g[h][lo:lo + sub, :] - g[h][lo - 1:lo, :])).astype(BF16)
            a_rows[h].append(jnp.dot(qi, kti, preferred_element_type=F32))
    a = [jnp.concatenate(a_rows[h], axis=0) for h in heads]

    for half, same_block in halves:
        size = 2 * half
        pos = krow % size
        for h in heads:
            g3 = g[h].reshape(c // size, size, B_KEY_DIM)
            ref = jnp.broadcast_to(g3[:, half - 1:half, :], g3.shape).reshape(c, B_KEY_DIM)
            ql = (q[h] * jnp.exp2(jnp.where(pos >= half, g[h] - ref, -jnp.inf))).astype(BF16)
            kl = (k[h] * jnp.exp2(jnp.where(pos < half, ref - g[h], -jnp.inf))).astype(BF16)
            al = lax.dot_general(ql, kl, NT_DIMS, preferred_element_type=F32)
            a[h] = a[h] + jnp.where(same_block, al, 0.0)

    rmod = krow % band
    for d in range(band):
        for h in heads:
            ks = k[h] if d == 0 else pltpu.roll(k[h], d, axis=0)
            gs = g[h] if d == 0 else pltpu.roll(g[h], d, axis=0)
            e = jnp.where(rmod >= d, g[h] - gs, -jnp.inf)
            r = jnp.dot((q[h] * ks * jnp.exp2(e)).astype(BF16), ones, preferred_element_type=F32)
            a[h] = a[h] + jnp.where(diff == -d, r, 0.0)

    for h in heads:
        oh = o[h] + jnp.dot(a[h].astype(BF16), v[h], preferred_element_type=F32)
        glast = gt[h][:, c - 1:c]
        kend = (kt[h] * jnp.exp2(glast - gt[h])).astype(BF16)
        s_ref[h] = jnp.exp2(glast) * s_ref[h] + jnp.dot(kend, v[h], preferred_element_type=F32)
        gb = gb_ref[:, vcols[h]]
        o_ref[:, vcols[h]] = (_rms(oh, gh_ref[...]) * (gb * _sigmoid(gb))).astype(o_ref.dtype)


def _gla(qb, kb, vb, gb, small, wup_pad, b_gate, g_head, B, L):
    c = GLA_CHUNK
    nc = L // c
    row = lambda b, i: (b * nc + i, 0)
    fixed = lambda b, i: (0, 0)
    return pl.pallas_call(
        functools.partial(_gla_kernel, c=c, sub=GLA_SUB, band=GLA_BAND),
        grid=(B, nc),
        in_specs=[pl.BlockSpec((c, B_KEY_WIDTH), row),
                  pl.BlockSpec((c, B_KEY_WIDTH), row),
                  pl.BlockSpec((c, B_VAL_WIDTH), row),
                  pl.BlockSpec((c, B_VAL_WIDTH), row),
                  pl.BlockSpec((c, LANES), row),
                  pl.BlockSpec((LANES, B_KEY_WIDTH), fixed),
                  pl.BlockSpec((1, B_KEY_WIDTH), fixed),
                  pl.BlockSpec((1, B_VAL_DIM), fixed)],
        out_specs=pl.BlockSpec((c, B_VAL_WIDTH), row),
        out_shape=jax.ShapeDtypeStruct((B * L, B_VAL_WIDTH), BF16),
        scratch_shapes=[pltpu.VMEM((B_HEADS, B_KEY_DIM, B_VAL_DIM), F32)],
        compiler_params=_params("arbitrary", "arbitrary"),
        name="gla",
    )(qb, kb, vb, gb, small, wup_pad, b_gate.reshape(1, B_KEY_WIDTH), g_head.reshape(1, B_VAL_DIM))


def _merge_kernel(x_ref, p_ref, oa_ref, ob_ref, ma_ref, mb_ref, wpa_ref, wpb_ref, wout_ref,
                  gpost_ref, wple_ref, wpg_ref, gpre_ref, gpost2_ref, o_ref, *, parts):
    rows = x_ref.shape[0] // parts
    sl = [slice(i * rows, (i + 1) * rows) for i in range(parts)]
    dot = lambda a, w_ref: jnp.dot(a, w_ref[...], preferred_element_type=F32)
    ya = [dot(oa_ref[r, :], wpa_ref) for r in sl]
    yb = [dot(ob_ref[r, :], wpb_ref) for r in sl]
    y = [_sigmoid(ma_ref[r, :]) * a + _sigmoid(mb_ref[r, :]) * b for r, a, b in zip(sl, ya, yb)]
    u = [dot(t.astype(BF16), wout_ref) for t in y]
    x1 = [x_ref[r, :] + _rms(t, gpost_ref[...]) for r, t in zip(sl, u)]
    gate = [dot(_rms(t, gpre_ref[...]).astype(BF16), wpg_ref) for t in x1]
    e = [dot(p_ref[r, :].astype(BF16), wple_ref) * _sigmoid(t) for r, t in zip(sl, gate)]
    for r, a, b in zip(sl, x1, e):
        o_ref[r, :] = a + _rms(b, gpost2_ref[...])


def _merge(x2d, p2d, oa, ob, ma, mb, wpa, wpb, wout, g_post, wple, wpg, g_pre2, g_post2):
    T = x2d.shape[0]
    tm = MERGE_ROWS
    rows = lambda width: pl.BlockSpec((tm, width), lambda i: (i, 0))
    full = lambda a: pl.BlockSpec(a.shape, lambda i: (0, 0))
    vec = lambda a: a.reshape(1, D_MODEL)
    args = (x2d, p2d, oa, ob, ma, mb, wpa, wpb, wout, vec(g_post), wple, wpg, vec(g_pre2), vec(g_post2))
    in_specs = [rows(D_MODEL), rows(PLE_DIM), rows(A_WIDTH), rows(B_VAL_WIDTH), rows(D_MODEL), rows(D_MODEL)]
    in_specs += [full(a) for a in args[6:]]
    return pl.pallas_call(
        functools.partial(_merge_kernel, parts=MERGE_PARTS),
        grid=(T // tm,),
        in_specs=in_specs,
        out_specs=rows(D_MODEL),
        out_shape=jax.ShapeDtypeStruct((T, D_MODEL), F32),
        compiler_params=_params("arbitrary"),
        name="merge",
    )(*args)


def _split_w_in(w_in):
    offs = [0]
    for s in IN_SPLITS:
        offs.append(offs[-1] + s)
    col = lambda i: w_in[:, offs[i]:offs[i + 1]]
    qa, ka, va, qi, ki, wi, ga, qb, kb, vb, gdown, gb, ma, mb = [col(i) for i in range(len(IN_SPLITS))]
    pad = jnp.zeros((D_MODEL, LANES - GATE_RANK), w_in.dtype)
    small = jnp.concatenate([gdown, pad], axis=1)
    groups = [qa.T, ka, va.T, qi.T, ki, wi.T, small, ga, qb, kb, vb, gb, ma, mb]
    dtypes = [BF16, BF16, BF16, BF16, BF16, F32, F32, F32, F32, F32, BF16, F32, F32, F32]
    transposed = [True, False, True, True, False, True] + [False] * 8
    return [g.astype(BF16) for g in groups], dtypes, transposed


def _layer(x2d, p2d, B, L, g_pre, w_in, w_gate_up, b_gate, g_gla_head, w_proj_a, w_proj_b,
           w_out, g_post, w_ple, w_ple_gate, g_ple_pre, g_ple_post):
    groups, dtypes, transposed = _split_w_in(w_in)
    qat, ka, vat, qit, ki, wit, small, ga, qb, kb, vb, gb, ma, mb = _in_proj(
        x2d, g_pre, groups, dtypes, transposed)
    oa = _dsa(qit, wit, qat, ga, ki, ka, vat, B, L)
    wup_pad = jnp.zeros((LANES, B_KEY_WIDTH), F32).at[SMALL_GD_OFF:SMALL_GD_OFF + GATE_RANK].set(w_gate_up)
    ob = _gla(qb, kb, vb, gb, small, wup_pad, b_gate, g_gla_head, B, L)
    bf = lambda a: a.astype(BF16)
    return _merge(x2d, p2d, oa, ob, ma, mb, bf(w_proj_a), bf(w_proj_b), bf(w_out), g_post,
                  bf(w_ple), bf(w_ple_gate), g_ple_pre, g_ple_post)


def kernel(x, p, g_pre, w_in, w_gate_up, b_gate, g_gla_head, w_proj_a, w_proj_b, w_out, g_post,
           w_ple, w_ple_gate, g_ple_pre, g_ple_post):
    B, L, _ = x.shape
    depth = p.shape[0]
    x2d = x.reshape(B * L, D_MODEL)
    for i in range(depth):
        x2d = _layer(x2d, p[i].reshape(B * L, PLE_DIM), B, L, g_pre[i], w_in[i], w_gate_up[i],
                     b_gate[i], g_gla_head[i], w_proj_a[i], w_proj_b[i], w_out[i], g_post[i],
                     w_ple[i], w_ple_gate[i], g_ple_pre[i], g_ple_post[i])
    return x2d.reshape(B, L, D_MODEL)
```
